```python
import math
import jax, jax.numpy as jnp
from jax import lax
import numpy as np

D_MODEL = 2048
BATCH = 2
SEQ = 4096
DEPTH = 1
DEC_BATCH = 128
DEC_SEQ = 4
PAST_LEN = 2048
PAGE_SIZE = 128

HEAD_DIM = 64
N_HEADS = D_MODEL // 128
KV_HEADS = 4
Q_PER_KV = N_HEADS // KV_HEADS
ATT_W = N_HEADS * HEAD_DIM
KV_W = KV_HEADS * HEAD_DIM
ATT_SCALE = HEAD_DIM ** -0.5
IDX_HEADS = N_HEADS // 2
IDX_DIM = 64
IDX_SCALE = (IDX_HEADS * IDX_DIM) ** -0.5
TOPK_MAX = 256
Q_BLOCK = 128
CHUNK = 128
GMLP_W = D_MODEL // 2
GMLP_GROUPS = 8
GMLP_GDIM = GMLP_W // GMLP_GROUPS
N_GROUPS = 4
EXP_PER_GROUP = 4
N_EXPERTS = N_GROUPS * EXP_PER_GROUP
TOP_K_FINE = 2
EXPERT_FF = D_MODEL // 4
PLE_DIM = 256
EPS = 1e-6

IN_SIZES = (ATT_W, KV_W, KV_W, IDX_HEADS * IDX_DIM, IDX_DIM, IDX_HEADS,
            GMLP_W, GMLP_W, D_MODEL, D_MODEL)
IN_COLS = sum(IN_SIZES)

kernel_name = 'dsa_gmlp_hiermoe_hybrid_step'


def rmsnorm(x, g):
    xf = x.astype(jnp.float32)
    y = xf * lax.rsqrt(jnp.mean(xf * xf, axis=-1, keepdims=True) + EPS)
    return y.astype(x.dtype) * g


def layernorm(x, g, b):
    xf = x.astype(jnp.float32)
    mu = jnp.mean(xf, axis=-1, keepdims=True)
    xc = xf - mu
    y = xc * lax.rsqrt(jnp.mean(xc * xc, axis=-1, keepdims=True) + EPS)
    return y.astype(x.dtype) * g + b


def split_in(z):
    parts, off = [], 0
    for n in IN_SIZES:
        parts.append(z[..., off:off + n])
        off += n
    return parts


def project_heads(xn, w):
    q, k, v, qi, ki, wi, gu, gv, ga, gb = split_in(xn @ w)
    lead = xn.shape[:-1]
    q = q.reshape(lead + (KV_HEADS, Q_PER_KV, HEAD_DIM))
    k = k.reshape(lead + (KV_HEADS, HEAD_DIM))
    v = v.reshape(lead + (KV_HEADS, HEAD_DIM))
    qi = qi.reshape(lead + (IDX_HEADS, IDX_DIM))
    return (q, k, v, qi, ki, wi * IDX_SCALE, jax.nn.gelu(gu), jax.nn.gelu(gv), ga, gb)


def indexer_scores(qi, wi, ki, tpos, spos):
    s = jnp.einsum('bthd,bsd->bths', qi, ki)
    s = jnp.einsum('bths,bth->bts', jax.nn.relu(s), wi).astype(jnp.float32)
    causal = spos[None, :] <= tpos[:, None]
    return jnp.where(causal[None], s, -jnp.inf)


def sparse_attend(q, ks, vs, valid):
    b, t = q.shape[:2]
    logits = jnp.einsum('btgqd,btkgd->btgqk', q, ks).astype(jnp.float32) * ATT_SCALE
    logits = jnp.where(valid[:, :, None, None, :], logits, -jnp.inf)
    p = jax.nn.softmax(logits, axis=-1).astype(vs.dtype)
    o = jnp.einsum('btgqk,btkgd->btgqd', p, vs)
    return o.reshape(b, t, ATT_W)


def prompt_sparse_attention(q, qi, wi, k, v, ki, ktop):
    B, S = q.shape[:2]
    nb = S // Q_BLOCK
    spos = jnp.arange(S)
    bidx = jnp.arange(B)[:, None, None]

    def block(args):
        qb, qib, wib, t0 = args
        tpos = t0 + jnp.arange(Q_BLOCK)
        sc = indexer_scores(qib, wib, ki, tpos, spos)
        _, idx = lax.top_k(sc, ktop)
        valid = idx <= tpos[None, :, None]
        return sparse_attend(qb, k[bidx, idx], v[bidx, idx], valid)

    blocks = lambda a: a.reshape((B, nb, Q_BLOCK) + a.shape[2:]).swapaxes(0, 1)
    out = lax.map(block, (blocks(q), blocks(qi), blocks(wi), jnp.arange(nb) * Q_BLOCK))
    return out.swapaxes(0, 1).reshape(B, S, ATT_W)


def sample_sparse_attention(q, qi, wi, k_new, v_new, ki_new, cache_k, cache_v,
                            cache_kidx, page_table, layer, ktop):
    DB, DS = q.shape[:2]
    past = page_table.shape[1] * PAGE_SIZE
    ki_past = cache_kidx[page_table, layer].reshape(DB, past, IDX_DIM)
    ki_all = jnp.concatenate([ki_past, ki_new.astype(ki_past.dtype)], axis=1)
    tpos = past + jnp.arange(DS)
    sc = indexer_scores(qi, wi, ki_all, tpos, jnp.arange(past + DS))
    _, idx = lax.top_k(sc, ktop)
    bidx = jnp.arange(DB)[:, None, None]
    pidx = jnp.minimum(idx, past - 1)
    phys = page_table[bidx, pidx // PAGE_SIZE]
    off = pidx % PAGE_SIZE
    nidx = jnp.clip(idx - past, 0, DS - 1)
    in_past = (idx < past)[..., None, None]
    ks = jnp.where(in_past, cache_k[phys, layer, off], k_new[bidx, nidx])
    vs = jnp.where(in_past, cache_v[phys, layer, off], v_new[bidx, nidx])
    valid = idx <= tpos[None, :, None]
    return sparse_attend(q, ks, vs, valid)


def spatial_gate(u, v, ws, bs, ln_g, ln_b):
    n = v.shape[-2]
    vn = layernorm(v, ln_g, ln_b)
    vg = vn.reshape(vn.shape[:-1] + (GMLP_GROUPS, GMLP_GDIM))
    w = jnp.tril(ws[:, :n, :n])
    mixed = jnp.einsum('gts,...sgc->...tgc', w, vg) + bs[:, :n].T[:, :, None]
    return u * mixed.reshape(u.shape), vn


def hier_moe(x, w_rg, b_rg, w_re, b_re, w_gate, w_up, w_down):
    T = x.shape[0]
    lg = (x @ w_rg).astype(jnp.float32) + b_rg
    grp = jnp.argmax(lg, axis=-1)
    pg = jnp.take_along_axis(jax.nn.softmax(lg, axis=-1), grp[:, None], axis=-1)
    le = ((x @ w_re).astype(jnp.float32) + b_re).reshape(T, N_GROUPS, EXP_PER_GROUP)
    le = jnp.take_along_axis(le, grp[:, None, None], axis=1)[:, 0]
    top_v, top_i = lax.top_k(le, TOP_K_FINE)
    wf = jax.nn.softmax(top_v, axis=-1) * pg
    eid = grp[:, None] * EXP_PER_GROUP + top_i
    combine = jnp.sum(jax.nn.one_hot(eid, N_EXPERTS, dtype=jnp.float32) * wf[..., None], axis=1)
    h = jax.nn.silu(jnp.einsum('td,edf->tef', x, w_gate)) * jnp.einsum('td,edf->tef', x, w_up)
    h = h * combine[:, :, None].astype(h.dtype)
    return jnp.einsum('tef,efd->td', h, w_down)


def setup_inputs(seed: int = 0) -> dict:
    key = jax.random.key(seed)
    keys = jax.random.split(key, 40)
    ctr = [0]

    def nk():
        ctr[0] += 1
        return keys[ctr[0] - 1]

    f32 = jnp.float32
    nrm = lambda shape, scale: jax.random.normal(nk(), shape, f32) * scale
    gain = lambda shape: 1.0 + 0.02 * jax.random.normal(nk(), shape, f32)

    n_pages = PAST_LEN // PAGE_SIZE
    n_used = DEC_BATCH * n_pages
    n_phys = (5 * n_used) // 4
    perm = jax.random.permutation(nk(), n_phys).astype(jnp.int32)
    page_table = perm[:n_used].reshape(DEC_BATCH, n_pages)

    return {
        'x_prompt': nrm((BATCH, SEQ, D_MODEL), 1.0),
        'x_sample': nrm((DEC_BATCH, DEC_SEQ, D_MODEL), 1.0),
        'cache_k': nrm((n_phys, DEPTH, PAGE_SIZE, KV_HEADS, HEAD_DIM), 1.0),
        'cache_v': nrm((n_phys, DEPTH, PAGE_SIZE, KV_HEADS, HEAD_DIM), 1.0),
        'cache_kidx': nrm((n_phys, DEPTH, PAGE_SIZE, IDX_DIM), 1.0),
        'page_table': page_table,
        'p_prompt': nrm((DEPTH, BATCH, SEQ, PLE_DIM), 1.0),
        'p_sample': nrm((DEPTH, DEC_BATCH, DEC_SEQ, PLE_DIM), 1.0),
        'g_mix': gain((DEPTH, D_MODEL)),
        'w_in': nrm((DEPTH, D_MODEL, IN_COLS), D_MODEL ** -0.5),
        'gmlp_ln_g': gain((DEPTH, GMLP_W)),
        'gmlp_ln_b': nrm((DEPTH, GMLP_W), 0.02),
        'gmlp_ws': nrm((DEPTH, GMLP_GROUPS, CHUNK, CHUNK), CHUNK ** -0.5),
        'gmlp_b': 1.0 + nrm((DEPTH, GMLP_GROUPS, CHUNK), 0.1),
        'w_pa': nrm((DEPTH, ATT_W, D_MODEL), ATT_W ** -0.5),
        'w_pb': nrm((DEPTH, GMLP_W, D_MODEL), GMLP_W ** -0.5),
        'w_o': nrm((DEPTH, D_MODEL, D_MODEL), D_MODEL ** -0.5),
        'g_ffn': gain((DEPTH, D_MODEL)),
        'w_router_g': nrm((DEPTH, D_MODEL, N_GROUPS), D_MODEL ** -0.5),
        'b_router_g': nrm((DEPTH, N_GROUPS), 0.01),
        'w_router_e': nrm((DEPTH, D_MODEL, N_EXPERTS), D_MODEL ** -0.5),
        'b_router_e': nrm((DEPTH, N_EXPERTS), 0.01),
        'w_e_gate': nrm((DEPTH, N_EXPERTS, D_MODEL, EXPERT_FF), D_MODEL ** -0.5),
        'w_e_up': nrm((DEPTH, N_EXPERTS, D_MODEL, EXPERT_FF), D_MODEL ** -0.5),
        'w_e_down': nrm((DEPTH, N_EXPERTS, EXPERT_FF, D_MODEL), EXPERT_FF ** -0.5),
        'g_ple': gain((DEPTH, D_MODEL)),
        'w_ple_gate': nrm((DEPTH, D_MODEL, D_MODEL), D_MODEL ** -0.5),
        'w_ple_proj': nrm((DEPTH, PLE_DIM, D_MODEL), PLE_DIM ** -0.5),
        'g_final': gain((D_MODEL,)),
    }


def reference(x_prompt, x_sample, cache_k, cache_v, cache_kidx, page_table, p_prompt, p_sample,
              g_mix, w_in, gmlp_ln_g, gmlp_ln_b, gmlp_ws, gmlp_b, w_pa, w_pb, w_o, g_ffn,
              w_router_g, b_router_g, w_router_e, b_router_e, w_e_gate, w_e_up, w_e_down,
              g_ple, w_ple_gate, w_ple_proj, g_final):
    B, S = x_prompt.shape[:2]
    DB, DS = x_sample.shape[:2]
    past = page_table.shape[1] * PAGE_SIZE
    ktop_prompt = min(TOPK_MAX, S // 4)
    ktop_sample = min(TOPK_MAX, (past + DS) // 4)

    hp, hs = x_prompt, x_sample
    kp_l, vp_l, kip_l, ks_l, vs_l, kis_l, gvs_l = [], [], [], [], [], [], []
    for i in range(DEPTH):
        def tail(h, a_out, b_out, ga, gb, p_l):
            m = jax.nn.sigmoid(ga) * (a_out @ w_pa[i]) + jax.nn.sigmoid(gb) * (b_out @ w_pb[i])
            h = h + m @ w_o[i]
            hn = rmsnorm(h, g_ffn[i])
            y = hier_moe(hn.reshape(-1, D_MODEL), w_router_g[i], b_router_g[i], w_router_e[i],
                         b_router_e[i], w_e_gate[i], w_e_up[i], w_e_down[i])
            h = h + y.reshape(h.shape)
            gate = jax.nn.sigmoid(rmsnorm(h, g_ple[i]) @ w_ple_gate[i])
            return h + gate * (p_l @ w_ple_proj[i])

        q, k, v, qi, ki, wi, gu, gv, ga, gb = project_heads(rmsnorm(hp, g_mix[i]), w_in[i])
        a_out = prompt_sparse_attention(q, qi, wi, k, v, ki, ktop_prompt)
        nc = S // CHUNK
        b_out, _ = spatial_gate(gu.reshape(B, nc, CHUNK, GMLP_W), gv.reshape(B, nc, CHUNK, GMLP_W),
                                gmlp_ws[i], gmlp_b[i], gmlp_ln_g[i], gmlp_ln_b[i])
        hp = tail(hp, a_out, b_out.reshape(B, S, GMLP_W), ga, gb, p_prompt[i])
        kp_l.append(k)
        vp_l.append(v)
        kip_l.append(ki)

        q, k, v, qi, ki, wi, gu, gv, ga, gb = project_heads(rmsnorm(hs, g_mix[i]), w_in[i])
        a_out = sample_sparse_attention(q, qi, wi, k, v, ki, cache_k, cache_v, cache_kidx,
                                        page_table, i, ktop_sample)
        b_out, gvn = spatial_gate(gu, gv, gmlp_ws[i], gmlp_b[i], gmlp_ln_g[i], gmlp_ln_b[i])
        hs = tail(hs, a_out, b_out, ga, gb, p_sample[i])
        ks_l.append(k)
        vs_l.append(v)
        kis_l.append(ki)
        gvs_l.append(gvn)

    y_prompt = rmsnorm(hp, g_final)
    y_sample = rmsnorm(hs, g_final)
    new_k_prompt = jnp.stack(kp_l, axis=1)
    new_v_prompt = jnp.stack(vp_l, axis=1)
    new_kidx_prompt = jnp.stack(kip_l, axis=1)
    new_k_sample = jnp.stack(ks_l, axis=1)
    new_v_sample = jnp.stack(vs_l, axis=1)
    new_kidx_sample = jnp.stack(kis_l, axis=1)
    new_gmlp_v_sample = jnp.stack(gvs_l, axis=1)
    return (y_prompt, y_sample, new_k_prompt, new_v_prompt, new_kidx_prompt,
            new_k_sample, new_v_sample, new_kidx_sample, new_gmlp_v_sample)
```

```python
import functools

import jax
import jax.numpy as jnp
from jax import lax
from jax.experimental import pallas as pl
from jax.experimental.pallas import tpu as pltpu

F32 = jnp.float32
BF16 = jnp.bfloat16
I32 = jnp.int32

D_MODEL = 2048
HEAD_DIM = 64
N_HEADS = 16
KV_HEADS = 4
Q_PER_KV = N_HEADS // KV_HEADS
ATT_W = N_HEADS * HEAD_DIM
KV_W = KV_HEADS * HEAD_DIM
ATT_SCALE = HEAD_DIM ** -0.5
IDX_HEADS = 8
IDX_DIM = 64
IDX_SCALE = (IDX_HEADS * IDX_DIM) ** -0.5
TOPK_MAX = 256
CHUNK = 128
GMLP_W = D_MODEL // 2
GMLP_GROUPS = 8
N_GROUPS = 4
EXP_PER_GROUP = 4
N_EXPERTS = N_GROUPS * EXP_PER_GROUP
EXPERT_FF = D_MODEL // 4
PLE_DIM = 256
EPS = 1e-6
PAGE_SIZE = 128

LANES = 128
VMEM_LIMIT_BYTES = 56 * 2 ** 20
MASKED = -1e30
INT_MIN = -2 ** 31
PROJ_COLS = 17 * LANES
PROJ_USED = ATT_W + 2 * KV_W + IDX_HEADS * IDX_DIM + IDX_DIM + IDX_HEADS
NEW_PAD = LANES

_NT = (((1,), (1,)), ((), ()))


def _params(n_axes):
    return pltpu.CompilerParams(dimension_semantics=("arbitrary",) * n_axes,
                                vmem_limit_bytes=VMEM_LIMIT_BYTES)


def _const_spec(shape):
    return pl.BlockSpec(shape, lambda *_: (0,) * len(shape), pipeline_mode=pl.Buffered(1))


def _rmsnorm(x, g):
    return (x * lax.rsqrt(jnp.mean(x * x, axis=-1, keepdims=True) + EPS)) * g


def _gelu(x):
    return x * (0.5 * (1.0 + jnp.tanh(0.7978845608028654 * (x + 0.044715 * (x * x * x)))))


def _sigmoid(x):
    return 1.0 / (1.0 + jnp.exp(-x))


def _dot(a, b):
    return jnp.dot(a, b, preferred_element_type=F32)


def _dot_nt(a, b):
    return lax.dot_general(a, b, _NT, preferred_element_type=F32)


def _proj_kernel(x_ref, g_ref, w_ref, xn_ref, q_ref, k_ref, v_ref, qi_ref, kiwi_ref):
    xnb = _rmsnorm(x_ref[...], g_ref[...]).astype(BF16)
    xn_ref[...] = xnb
    o = 0
    q_ref[...] = (_dot(xnb, w_ref[:, o:o + ATT_W]) * ATT_SCALE).astype(BF16)
    o += ATT_W
    k_ref[...] = _dot(xnb, w_ref[:, o:o + KV_W])
    o += KV_W
    v_ref[...] = _dot(xnb, w_ref[:, o:o + KV_W])
    o += KV_W
    qi_ref[...] = _dot(xnb, w_ref[:, o:o + IDX_HEADS * IDX_DIM]).astype(BF16)
    o += IDX_HEADS * IDX_DIM
    kiwi_ref[...] = _dot(xnb, w_ref[:, o:o + LANES])


def _proj(x, g, w, tm):
    t = x.shape[0]
    row = lambda n: pl.BlockSpec((tm, n), lambda i: (i, 0))
    return pl.pallas_call(
        _proj_kernel,
        grid=(t // tm,),
        in_specs=[row(D_MODEL), _const_spec((1, D_MODEL)), _const_spec((D_MODEL, PROJ_COLS))],
        out_specs=[row(D_MODEL), row(ATT_W), row(KV_W), row(KV_W), row(IDX_HEADS * IDX_DIM), row(LANES)],
        out_shape=[jax.ShapeDtypeStruct((t, D_MODEL), BF16), jax.ShapeDtypeStruct((t, ATT_W), BF16),
                   jax.ShapeDtypeStruct((t, KV_W), F32), jax.ShapeDtypeStruct((t, KV_W), F32),
                   jax.ShapeDtypeStruct((t, IDX_HEADS * IDX_DIM), BF16), jax.ShapeDtypeStruct((t, LANES), F32)],
        compiler_params=_params(1), name="proj",
    )(x, g, w)


def _gmlp_kernel(xn_ref, w_ref, lng_ref, lnb_ref, ws_ref, bias_ref, out_ref, *gvn_ref, n_chunks):
    xnb = xn_ref[...]
    gu = _gelu(_dot(xnb, w_ref[:, :GMLP_W]))
    gv = _gelu(_dot(xnb, w_ref[:, GMLP_W:]))
    xc = gv - jnp.mean(gv, axis=-1, keepdims=True)
    vn = xc * lax.rsqrt(jnp.mean(xc * xc, axis=-1, keepdims=True) + EPS) * lng_ref[...] + lnb_ref[...]
    if gvn_ref:
        gvn_ref[0][...] = vn
    vnb = vn.astype(BF16)
    r = lax.broadcasted_iota(I32, (CHUNK, CHUNK), 0)
    c = lax.broadcasted_iota(I32, (CHUNK, CHUNK), 1)
    gd = GMLP_W // GMLP_GROUPS
    for g in range(GMLP_GROUPS):
        wg = jnp.where(c <= r, ws_ref[g], 0.0).astype(BF16)
        rhs = jnp.concatenate([vnb[ch * CHUNK:(ch + 1) * CHUNK, g * gd:(g + 1) * gd] for ch in range(n_chunks)],
                              axis=1)
        mixed = _dot(wg, rhs)
        b = bias_ref[:, g:g + 1]
        for ch in range(n_chunks):
            rs = slice(ch * CHUNK, (ch + 1) * CHUNK)
            out_ref[rs, g * gd:(g + 1) * gd] = (gu[rs, g * gd:(g + 1) * gd]
                                                * (mixed[:, ch * gd:(ch + 1) * gd] + b)).astype(BF16)


def _gmlp(xn, w, lng, lnb, ws, bias, tm, want_vn):
    t = xn.shape[0]
    row = lambda n: pl.BlockSpec((tm, n), lambda i: (i, 0))
    out_specs = [row(GMLP_W)]
    out_shape = [jax.ShapeDtypeStruct((t, GMLP_W), BF16)]
    if want_vn:
        out_specs.append(row(GMLP_W))
        out_shape.append(jax.ShapeDtypeStruct((t, GMLP_W), F32))
    return pl.pallas_call(
        functools.partial(_gmlp_kernel, n_chunks=tm // CHUNK),
        grid=(t // tm,),
        in_specs=[row(D_MODEL), _const_spec((D_MODEL, 2 * GMLP_W)), _const_spec((1, GMLP_W)),
                  _const_spec((1, GMLP_W)), _const_spec((GMLP_GROUPS, CHUNK, CHUNK)),
                  _const_spec((CHUNK, GMLP_GROUPS))],
        out_specs=out_specs, out_shape=out_shape,
        compiler_params=_params(1), name="gmlp",
    )(xn, w, lng, lnb, ws, bias)


def _score_key(s):
    bits = lax.bitcast_convert_type(s, I32)
    key = jnp.where(bits < 0, bits ^ jnp.int32(0x7FFFFFFF), bits)
    return jnp.where(bits == jnp.int32(INT_MIN), 0, key)


def _count(key_ref, nkb, kb, pred):
    rows = key_ref.shape[0]

    def body(cb, acc):
        off = pl.multiple_of(cb * kb, kb)
        for i in range(kb // LANES):
            blk = key_ref[:, pl.ds(off + i * LANES, LANES)]
            acc = acc + jnp.where(pred(blk, off + i * LANES), 1.0, 0.0)
        return acc

    acc = lax.fori_loop(0, nkb, body, jnp.zeros((rows, LANES), F32))
    return jnp.sum(acc, axis=1, keepdims=True)


def _select_topk(key_ref, ju_ref, nkb, kb, ktop, n_cols):
    rows = key_ref.shape[0]
    bc = lambda x: jnp.broadcast_to(x, (rows, LANES))

    def vstep(i, tu):
        cand_u = tu | jnp.left_shift(jnp.int32(1), 31 - i)
        cand = bc(cand_u ^ jnp.int32(INT_MIN))
        cnt = _count(key_ref, nkb, kb, lambda blk, off: blk >= cand)
        return jnp.where(cnt >= ktop, cand_u, tu)

    tu = lax.fori_loop(0, 32, vstep, jnp.zeros((rows, 1), I32))
    ts = tu ^ jnp.int32(INT_MIN)
    tsb = bc(ts)
    cnt_gt = _count(key_ref, nkb, kb, lambda blk, off: blk > tsb)
    cnt_ge = _count(key_ref, nkb, kb, lambda blk, off: blk >= tsb)
    need = ktop - cnt_gt
    tied = jnp.logical_and(cnt_ge > ktop, ts != jnp.int32(INT_MIN))
    ju_ref[...] = jnp.full((rows, 1), n_cols, I32)

    @pl.when(jnp.max(jnp.where(tied, 1.0, 0.0)) > 0.0)
    def _():
        lane = lax.broadcasted_iota(I32, (rows, LANES), 1)

        def jstep(i, ju):
            cand = ju | jnp.left_shift(jnp.int32(1), (n_cols - 1).bit_length() - 1 - i)
            cb_ = bc(cand)
            below = _count(key_ref, nkb, kb,
                           lambda blk, off: jnp.logical_and(blk == tsb, (lane + off) < cb_))
            return jnp.where(below < need, cand, ju)

        ju = lax.fori_loop(0, (n_cols - 1).bit_length(), jstep, jnp.zeros((rows, 1), I32))
        ju_ref[...] = jnp.where(tied, ju, n_cols)

    return ts


def _selected(blk, cols, ts, ju):
    return jnp.logical_or(blk > ts, jnp.logical_and(blk == ts, cols <= ju))


def _attn_prompt_kernel(qi_ref, wi_ref, ki_ref, q_ref, kk_ref, vv_ref, o_ref,
                        key_ref, bias_ref, ju_ref, m_ref, l_ref, acc_ref, *, tq, kb, ktop, seq):
    t0 = pl.program_id(1) * tq
    nkb = (pl.program_id(1) + 1) * (tq // kb)
    rows = t0 + lax.broadcasted_iota(I32, (tq, kb), 0)
    lane_kb = lax.broadcasted_iota(I32, (tq, kb), 1)
    low_half = lax.broadcasted_iota(I32, (tq, LANES), 1) < HEAD_DIM

    wi = wi_ref[...] * IDX_SCALE

    def score_body(cb, _):
        off = pl.multiple_of(cb * kb, kb)
        kblk = ki_ref[pl.ds(off, kb), :]
        acc = jnp.zeros((tq, kb), F32)
        for p in range(IDX_HEADS // 2):
            slab = qi_ref[:, p * LANES:(p + 1) * LANES]
            for half in range(2):
                qh = jnp.where(low_half if half == 0 else jnp.logical_not(low_half), slab, jnp.zeros_like(slab))
                h = 2 * p + half
                acc = acc + wi[:, h:h + 1] * jnp.maximum(_dot_nt(qh, kblk), 0.0)
        key_ref[:, pl.ds(off, kb)] = jnp.where(off + lane_kb <= rows, _score_key(acc), jnp.int32(INT_MIN))
        return 0

    lax.fori_loop(0, nkb, score_body, 0)

    ts = _select_topk(key_ref, ju_ref, nkb, kb, ktop, seq)
    ju = ju_ref[...]

    def bias_body(cb, _):
        off = pl.multiple_of(cb * kb, kb)
        cols = off + lane_kb
        sel = jnp.logical_and(_selected(key_ref[:, pl.ds(off, kb)], cols, ts, ju), cols <= rows)
        bias_ref[:, pl.ds(off, kb)] = jnp.where(sel, 0.0, MASKED)
        return 0

    lax.fori_loop(0, nkb, bias_body, 0)

    for g in range(KV_HEADS):
        parts = []
        for hh in range(Q_PER_KV):
            hd = Q_PER_KV * g + hh
            slab = q_ref[:, (hd // 2) * LANES:(hd // 2 + 1) * LANES]
            keep = low_half if hd % 2 == 0 else jnp.logical_not(low_half)
            parts.append(jnp.where(keep, slab, jnp.zeros_like(slab)))
        qg = jnp.concatenate(parts, axis=0)
        m_ref[...] = jnp.full(m_ref.shape, MASKED, F32)
        l_ref[...] = jnp.zeros(l_ref.shape, F32)
        acc_ref[...] = jnp.zeros(acc_ref.shape, F32)

        def kv_body(cb, _):
            off = pl.multiple_of(cb * kb, kb)
            b = bias_ref[:, pl.ds(off, kb)]
            s = _dot_nt(qg, kk_ref[g, pl.ds(off, kb), :]) + jnp.concatenate([b] * Q_PER_KV, axis=0)
            m_old = m_ref[...]
            m_new = jnp.maximum(m_old, jnp.max(s, axis=1, keepdims=True))
            p = jnp.exp(s - m_new)
            alpha = jnp.exp(m_old - m_new)
            l_ref[...] = alpha * l_ref[...] + jnp.sum(p, axis=1, keepdims=True)
            acc_ref[...] = alpha * acc_ref[...] + _dot(p.astype(BF16), vv_ref[g, pl.ds(off, kb), :])
            m_ref[...] = m_new
            return 0

        lax.fori_loop(0, nkb, kv_body, 0)
        out = acc_ref[...] / l_ref[...]
        for pr in range(Q_PER_KV // 2):
            slab = jnp.concatenate([out[(2 * pr + half) * tq:(2 * pr + half + 1) * tq, :HEAD_DIM]
                                    for half in range(2)], axis=1)
            hd = Q_PER_KV * g + 2 * pr
            o_ref[:, hd * HEAD_DIM:(hd + 2) * HEAD_DIM] = slab.astype(BF16)


def _attn_prompt(qi, wi, ki2, q, kk, vv, ktop, tq, kb):
    b, s = q.shape[:2]
    blk = lambda n: pl.BlockSpec((None, tq, n), lambda i, j: (i, j, 0))
    return pl.pallas_call(
        functools.partial(_attn_prompt_kernel, tq=tq, kb=kb, ktop=ktop, seq=s),
        grid=(b, s // tq),
        in_specs=[blk(IDX_HEADS * IDX_DIM), blk(IDX_HEADS),
                  pl.BlockSpec((None, s, LANES), lambda i, j: (i, 0, 0)),
                  blk(ATT_W),
                  pl.BlockSpec((None, KV_HEADS, s, LANES), lambda i, j: (i, 0, 0, 0)),
                  pl.BlockSpec((None, KV_HEADS, s, LANES), lambda i, j: (i, 0, 0, 0))],
        out_specs=blk(ATT_W),
        out_shape=jax.ShapeDtypeStruct((b, s, ATT_W), BF16),
        scratch_shapes=[pltpu.VMEM((tq, s), I32), pltpu.VMEM((tq, s), F32), pltpu.VMEM((tq, 1), I32),
                        pltpu.VMEM((Q_PER_KV * tq, 1), F32), pltpu.VMEM((Q_PER_KV * tq, 1), F32),
                        pltpu.VMEM((Q_PER_KV * tq, LANES), F32)],
        compiler_params=_params(2), name="attn_prompt",
    )(qi, wi, ki2, q, kk, vv)


def _page_copies(pt_ref, cache, buf, sem, step, slot, sb, n_pages):
    copies = []
    for s in range(sb):
        for p in range(n_pages):
            page = pt_ref[(step * sb + s) * n_pages + p]
            copies.append(pltpu.make_async_copy(cache.at[page], buf.at[slot, s, pl.ds(p * PAGE_SIZE, PAGE_SIZE)],
                                                sem.at[slot]))
    return copies


def _prefetch_pages(pt_ref, caches, bufs, sems, sb, n_pages):
    i = pl.program_id(0)
    n = pl.num_programs(0)
    slot = lax.rem(i, 2)

    @pl.when(i == 0)
    def _():
        for cache, buf, sem in zip(caches, bufs, sems):
            for cp in _page_copies(pt_ref, cache, buf, sem, 0, 0, sb, n_pages):
                cp.start()

    @pl.when(i + 1 < n)
    def _():
        for cache, buf, sem in zip(caches, bufs, sems):
            for cp in _page_copies(pt_ref, cache, buf, sem, i + 1, 1 - slot, sb, n_pages):
                cp.start()

    for cache, buf, sem in zip(caches, bufs, sems):
        for cp in _page_copies(pt_ref, cache, buf, sem, i, slot, sb, n_pages):
            cp.wait()
    return slot


def _sample_index_kernel(pt_ref, qi_ref, wi_ref, kin_ref, cache_ref, key_ref, buf, sem, *, sb, n_pages, ds):
    slot = _prefetch_pages(pt_ref, (cache_ref,), (buf,), (sem,), sb, n_pages)
    past = n_pages * PAGE_SIZE
    rows = lax.broadcasted_iota(I32, (8, NEW_PAD), 0)
    lane = lax.broadcasted_iota(I32, (8, NEW_PAD), 1)
    real_row = lax.broadcasted_iota(I32, (8, past), 0) < ds
    for s in range(sb):
        qs = qi_ref[s]
        w = wi_ref[s] * IDX_SCALE
        x_past = jnp.maximum(_dot_nt(qs, buf[slot, s].astype(BF16)), 0.0) * w
        x_new = jnp.maximum(_dot_nt(qs, kin_ref[s]), 0.0) * w
        sc_past = x_past[0:8]
        sc_new = x_new[0:8]
        for h in range(1, IDX_HEADS):
            sc_past = sc_past + x_past[8 * h:8 * h + 8]
            sc_new = sc_new + x_new[8 * h:8 * h + 8]
        key_ref[s, :, :past] = jnp.where(real_row, _score_key(sc_past), jnp.int32(INT_MIN))
        key_ref[s, :, past:] = jnp.where(jnp.logical_and(lane <= rows, rows < ds), _score_key(sc_new),
                                         jnp.int32(INT_MIN))


def _sample_index(pt, qi, wi, kin, cache_kidx, sb, ds):
    db = qi.shape[0]
    n_pages = pt.shape[0] // db
    past = n_pages * PAGE_SIZE
    blk = lambda *shape: pl.BlockSpec((sb,) + shape, lambda i, pt_: (i,) + (0,) * len(shape))
    return pl.pallas_call(
        functools.partial(_sample_index_kernel, sb=sb, n_pages=n_pages, ds=ds),
        grid_spec=pltpu.PrefetchScalarGridSpec(
            num_scalar_prefetch=1, grid=(db // sb,),
            in_specs=[blk(IDX_HEADS * 8, IDX_DIM), blk(IDX_HEADS * 8, 1), blk(NEW_PAD, IDX_DIM),
                      pl.BlockSpec(memory_space=pl.ANY)],
            out_specs=blk(8, past + NEW_PAD),
            scratch_shapes=[pltpu.VMEM((2, sb, past, IDX_DIM), F32), pltpu.SemaphoreType.DMA((2,))]),
        out_shape=jax.ShapeDtypeStruct((db, 8, past + NEW_PAD), I32),
        compiler_params=_params(1), name="sample_index",
    )(pt, qi, wi, kin, cache_kidx)


def _sample_select_kernel(key_ref, bias_ref, ju_ref, *, ktop):
    rows, n_cols = key_ref.shape
    nkb = n_cols // LANES
    ts = _select_topk(key_ref, ju_ref, nkb, LANES, ktop, n_cols)
    ju = ju_ref[...]
    lane = lax.broadcasted_iota(I32, (rows, LANES), 1)
    for cb in range(nkb):
        blk = key_ref[:, cb * LANES:(cb + 1) * LANES]
        sel = jnp.logical_and(_selected(blk, lane + cb * LANES, ts, ju), blk != jnp.int32(INT_MIN))
        bias_ref[:, cb * LANES:(cb + 1) * LANES] = jnp.where(sel, 0.0, MASKED)


def _sample_select(keys, ktop, tr):
    r, w = keys.shape
    return pl.pallas_call(
        functools.partial(_sample_select_kernel, ktop=ktop),
        grid=(r // tr,),
        in_specs=[pl.BlockSpec((tr, w), lambda i: (i, 0))],
        out_specs=pl.BlockSpec((tr, w), lambda i: (i, 0)),
        out_shape=jax.ShapeDtypeStruct((r, w), F32),
        scratch_shapes=[pltpu.VMEM((tr, 1), I32)],
        compiler_params=_params(1), name="sample_select",
    )(keys)


def _sample_attn_kernel(pt_ref, q_ref, kn_ref, vn_ref, bias_ref, ck_ref, cv_ref, o_ref,
                        kbuf, vbuf, ksem, vsem, *, sb, n_pages, ds):
    slot = _prefetch_pages(pt_ref, (ck_ref, cv_ref), (kbuf, vbuf), (ksem, vsem), sb, n_pages)
    past = n_pages * PAGE_SIZE
    n_rows = ds * N_HEADS
    row_g = (lax.broadcasted_iota(I32, (n_rows, HEAD_DIM), 0) // Q_PER_KV) % KV_HEADS
    for s in range(sb):
        q = q_ref[s]
        b = bias_ref[s]
        bias = jnp.concatenate([jnp.broadcast_to(b[t:t + 1], (N_HEADS, past + NEW_PAD)) for t in range(ds)], axis=0)
        s_past = _dot_nt(q, kbuf[slot, s].astype(BF16)) + bias[:, :past]
        s_new = _dot_nt(q, kn_ref[s]) + bias[:, past:]
        m = jnp.maximum(jnp.max(s_past, axis=1, keepdims=True), jnp.max(s_new, axis=1, keepdims=True))
        p_past = jnp.exp(s_past - m)
        p_new = jnp.exp(s_new - m)
        l = jnp.sum(p_past, axis=1, keepdims=True) + jnp.sum(p_new, axis=1, keepdims=True)
        o = (_dot(p_past.astype(BF16), vbuf[slot, s].astype(BF16)) + _dot(p_new.astype(BF16), vn_ref[s])) / l
        out = jnp.zeros((n_rows, HEAD_DIM), F32)
        for g in range(KV_HEADS):
            out = jnp.where(row_g == g, o[:, g * HEAD_DIM:(g + 1) * HEAD_DIM], out)
        o_ref[s] = out


def _sample_attn(pt, qbd, kn, vn, bias, cache_k, cache_v, sb, ds):
    db = qbd.shape[0]
    n_pages = pt.shape[0] // db
    past = n_pages * PAGE_SIZE
    blk = lambda *shape: pl.BlockSpec((sb,) + shape, lambda i, pt_: (i,) + (0,) * len(shape))
    return pl.pallas_call(
        functools.partial(_sample_attn_kernel, sb=sb, n_pages=n_pages, ds=ds),
        grid_spec=pltpu.PrefetchScalarGridSpec(
            num_scalar_prefetch=1, grid=(db // sb,),
            in_specs=[blk(ds * N_HEADS, KV_W), blk(NEW_PAD, KV_W), blk(NEW_PAD, KV_W), blk(8, past + NEW_PAD),
                      pl.BlockSpec(memory_space=pl.ANY), pl.BlockSpec(memory_space=pl.ANY)],
            out_specs=blk(ds * N_HEADS, HEAD_DIM),
            scratch_shapes=[pltpu.VMEM((2, sb, past, KV_W), F32), pltpu.VMEM((2, sb, past, KV_W), F32),
                            pltpu.SemaphoreType.DMA((2,)), pltpu.SemaphoreType.DMA((2,))]),
        out_shape=jax.ShapeDtypeStruct((db, ds * N_HEADS, HEAD_DIM), F32),
        compiler_params=_params(1), name="sample_attn",
    )(pt, qbd, kn, vn, bias, cache_k, cache_v)


def _merge_kernel(xn_ref, a_ref, b_ref, wg_ref, wpa_ref, wpb_ref, m_ref):
    xnb = xn_ref[...]
    ga = _sigmoid(_dot(xnb, wg_ref[:, :D_MODEL]))
    m = ga * _dot(a_ref[...], wpa_ref[...])
    gb = _sigmoid(_dot(xnb, wg_ref[:, D_MODEL:]))
    m_ref[...] = (m + gb * _dot(b_ref[...], wpb_ref[...])).astype(BF16)


def _merge(xn, a, b, wg, wpa, wpb, tm):
    t = xn.shape[0]
    row = lambda n: pl.BlockSpec((tm, n), lambda i: (i, 0))
    return pl.pallas_call(
        _merge_kernel,
        grid=(t // tm,),
        in_specs=[row(D_MODEL), row(ATT_W), row(GMLP_W), _const_spec((D_MODEL, 2 * D_MODEL)),
                  _const_spec((ATT_W, D_MODEL)), _const_spec((GMLP_W, D_MODEL))],
        out_specs=row(D_MODEL),
        out_shape=jax.ShapeDtypeStruct((t, D_MODEL), BF16),
        compiler_params=_params(1), name="merge",
    )(xn, a, b, wg, wpa, wpb)


def _route_kernel(h_ref, m_ref, wo_ref, g_ref, wrh_ref, wrl_ref, br_ref,
                  h1_ref, hn_ref, meta_ref, cnt_ref, run_ref, *, tm):
    @pl.when(pl.program_id(0) == 0)
    def _():
        run_ref[...] = jnp.zeros(run_ref.shape, F32)

    h1 = h_ref[...] + _dot(m_ref[...], wo_ref[...])
    h1_ref[...] = h1
    hn = _rmsnorm(h1, g_ref[...])
    hn_ref[...] = hn
    hn_hi = hn.astype(BF16)
    hn_lo = (hn - hn_hi.astype(F32)).astype(BF16)
    lg = _dot(hn_hi, wrh_ref[...]) + _dot(hn_lo, wrh_ref[...]) + _dot(hn_hi, wrl_ref[...]) + br_ref[...]

    lane = lax.broadcasted_iota(I32, (tm, LANES), 1)
    ninf = jnp.float32(-jnp.inf)
    rmax = lambda x: jnp.max(x, axis=1, keepdims=True)
    rmin = lambda x: jnp.min(x, axis=1, keepdims=True)
    lane_f = lane.astype(F32)
    far = jnp.float32(LANES)
    glog = jnp.where(jnp.logical_and(lane >= N_EXPERTS, lane < N_EXPERTS + N_GROUPS), lg, ninf)
    gmax = rmax(glog)
    grp = rmin(jnp.where(glog == gmax, lane_f, far)) - N_EXPERTS
    pg = 1.0 / jnp.sum(jnp.exp(glog - gmax), axis=1, keepdims=True)
    lane_grp = jnp.right_shift(lane, EXP_PER_GROUP.bit_length() - 1).astype(F32)
    le = jnp.where(jnp.logical_and(lane < N_EXPERTS, lane_grp == grp), lg, ninf)
    v0 = rmax(le)
    i0 = rmin(jnp.where(le == v0, lane_f, far))
    le1 = jnp.where(lane_f == i0, ninf, le)
    v1 = rmax(le1)
    i1 = rmin(jnp.where(le1 == v1, lane_f, far))
    e1 = jnp.exp(v1 - v0)
    w0 = pg / (1.0 + e1)
    w1 = pg * e1 / (1.0 + e1)

    sel0 = lane_f == i0
    sel1 = lane_f == i1
    sel = jnp.where(jnp.logical_or(sel0, sel1), 1.0, 0.0)
    r = lax.broadcasted_iota(I32, (tm, tm), 0)
    c = lax.broadcasted_iota(I32, (tm, tm), 1)
    before = jnp.where(c < r, 1.0, 0.0).astype(BF16)
    rank = _dot(before, sel.astype(BF16)) + run_ref[...]
    rank0 = jnp.sum(jnp.where(sel0, rank, 0.0), axis=1, keepdims=True)
    rank1 = jnp.sum(jnp.where(sel1, rank, 0.0), axis=1, keepdims=True)
    run_ref[...] = run_ref[...] + jnp.sum(sel, axis=0, keepdims=True)
    cnt_ref[...] = run_ref[...]

    meta = jnp.where(lane == 0, i0, 0.0)
    meta = jnp.where(lane == 1, i1, meta)
    meta = jnp.where(lane == 2, rank0, meta)
    meta = jnp.where(lane == 3, rank1, meta)
    meta = jnp.where(lane == 4, w0, meta)
    meta_ref[...] = jnp.where(lane == 5, w1, meta)


def _route(h, m, wo, g, wrh, wrl, br, tm):
    t = h.shape[0]
    row = lambda n: pl.BlockSpec((tm, n), lambda i: (i, 0))
    return pl.pallas_call(
        functools.partial(_route_kernel, tm=tm),
        grid=(t // tm,),
        in_specs=[row(D_MODEL), row(D_MODEL), _const_spec((D_MODEL, D_MODEL)), _const_spec((1, D_MODEL)),
                  _const_spec((D_MODEL, LANES)), _const_spec((D_MODEL, LANES)), _const_spec((1, LANES))],
        out_specs=[row(D_MODEL), row(D_MODEL), row(LANES), pl.BlockSpec((1, LANES), lambda i: (0, 0))],
        out_shape=[jax.ShapeDtypeStruct((t, D_MODEL), F32), jax.ShapeDtypeStruct((t, D_MODEL), F32),
                   jax.ShapeDtypeStruct((t, LANES), F32), jax.ShapeDtypeStruct((1, LANES), F32)],
        scratch_shapes=[pltpu.VMEM((1, LANES), F32)],
        compiler_params=_params(1), name="route",
    )(h, m, wo, g, wrh, wrl, br)


def _row_copy_kernel(sidx_ref, didx_ref, src_ref, dst_in_ref, dst_ref, sem, *, chunk):
    del dst_in_ref
    i = pl.program_id(0)
    n = pl.num_programs(0)
    slot = lax.rem(i, 2)

    def row_copy(j, sl):
        return pltpu.make_async_copy(src_ref.at[pl.ds(sidx_ref[0, j], 1)], dst_ref.at[pl.ds(didx_ref[0, j], 1)],
                                     sem.at[sl])

    def start(j, _):
        row_copy(j, slot).start()
        return 0

    lax.fori_loop(0, chunk, start, 0)

    def drain(sl):
        def wait(j, _):
            row_copy(0, sl).wait()
            return 0
        lax.fori_loop(0, chunk, wait, 0)

    @pl.when(i > 0)
    def _():
        drain(1 - slot)

    @pl.when(i == n - 1)
    def _():
        drain(slot)


def _row_copy(src_idx, dst_idx, src, dst_init, chunk):
    n = src_idx.shape[0]
    idx_spec = pl.BlockSpec((None, 1, chunk), lambda i: (i, 0, 0), memory_space=pltpu.SMEM)
    return pl.pallas_call(
        functools.partial(_row_copy_kernel, chunk=chunk),
        grid=(n // chunk,),
        in_specs=[idx_spec, idx_spec, pl.BlockSpec(memory_space=pl.ANY), pl.BlockSpec(memory_space=pl.ANY)],
        out_specs=pl.BlockSpec(memory_space=pl.ANY),
        out_shape=jax.ShapeDtypeStruct(dst_init.shape, dst_init.dtype),
        scratch_shapes=[pltpu.SemaphoreType.DMA((2,))],
        input_output_aliases={3: 0},
        compiler_params=_params(1), name="row_copy",
    )(src_idx.reshape(n // chunk, 1, chunk), dst_idx.reshape(n // chunk, 1, chunk), src, dst_init)


def _experts_kernel(te_ref, nv_ref, x_ref, wg_ref, wu_ref, wd_ref, y_ref):
    del te_ref

    @pl.when(pl.program_id(0) < nv_ref[0])
    def _():
        x = x_ref[...].astype(BF16)
        hg = _dot(x, wg_ref[...])
        hu = _dot(x, wu_ref[...])
        y_ref[...] = _dot((hg * _sigmoid(hg) * hu).astype(BF16), wd_ref[...])

    @pl.when(pl.program_id(0) >= nv_ref[0])
    def _():
        y_ref[...] = jnp.zeros(y_ref.shape, F32)


def _experts(tile_expert, n_valid, xs, wg, wu, wd, tm):
    p = xs.shape[0]
    wspec = lambda a, b: pl.BlockSpec((None, a, b), lambda i, te, nv: (te[i], 0, 0))
    return pl.pallas_call(
        _experts_kernel,
        grid_spec=pltpu.PrefetchScalarGridSpec(
            num_scalar_prefetch=2, grid=(p // tm,),
            in_specs=[pl.BlockSpec((tm, D_MODEL), lambda i, te, nv: (i, 0)),
                      wspec(D_MODEL, EXPERT_FF), wspec(D_MODEL, EXPERT_FF), wspec(EXPERT_FF, D_MODEL)],
            out_specs=pl.BlockSpec((tm, D_MODEL), lambda i, te, nv: (i, 0))),
        out_shape=jax.ShapeDtypeStruct((p, D_MODEL), F32),
        compiler_params=_params(1), name="experts",
    )(tile_expert, n_valid, xs, wg, wu, wd)


def _ple_kernel(h1_ref, y_ref, meta_ref, p_ref, gp_ref, wg_ref, wp_ref, gf_ref, h3_ref, out_ref):
    meta = meta_ref[...]
    h2 = h1_ref[...] + meta[:, 4:5] * y_ref[0] + meta[:, 5:6] * y_ref[1]
    gate = _sigmoid(_dot(_rmsnorm(h2, gp_ref[...]).astype(BF16), wg_ref[...]))
    h3 = h2 + gate * _dot(p_ref[...], wp_ref[...])
    h3_ref[...] = h3
    out_ref[...] = _rmsnorm(h3, gf_ref[...])


def _ple(h1, y2, meta, p, gp, wg, wp, gf, tm):
    t = h1.shape[0]
    row = lambda n: pl.BlockSpec((tm, n), lambda i: (i, 0))
    return pl.pallas_call(
        _ple_kernel,
        grid=(t // tm,),
        in_specs=[row(D_MODEL), pl.BlockSpec((2, tm, D_MODEL), lambda i: (0, i, 0)), row(LANES), row(PLE_DIM),
                  _const_spec((1, D_MODEL)), _const_spec((D_MODEL, D_MODEL)), _const_spec((PLE_DIM, D_MODEL)),
                  _const_spec((1, D_MODEL))],
        out_specs=[row(D_MODEL), row(D_MODEL)],
        out_shape=[jax.ShapeDtypeStruct((t, D_MODEL), F32), jax.ShapeDtypeStruct((t, D_MODEL), F32)],
        compiler_params=_params(1), name="ple",
    )(h1, y2, meta, p, gp, wg, wp, gf)


def _dispatch_plan(meta, counts, tile):
    t = meta.shape[0]
    e = meta[:, 0:2].astype(I32)
    rank = meta[:, 2:4].astype(I32)
    cnt = counts[0, :N_EXPERTS].astype(I32)
    padded = ((cnt + tile - 1) // tile) * tile
    ends = jnp.cumsum(padded)
    starts = ends - padded
    pos = starts[e] + rank
    n_tiles = (2 * t) // tile + N_EXPERTS
    tile_start = jnp.arange(n_tiles, dtype=I32) * tile
    tile_expert = jnp.minimum(jnp.sum((tile_start[:, None] >= ends[None, :]).astype(I32), axis=1), N_EXPERTS - 1)
    n_valid = (ends[-1] // tile).astype(I32).reshape(1)
    last = tile_expert[jnp.maximum(n_valid[0] - 1, 0)]
    tile_expert = jnp.where(jnp.arange(n_tiles) < n_valid[0], tile_expert, last)
    return pos, tile_expert, n_valid, n_tiles


def _tail(h, xn, a_out, b_out, p_l, w, tm, moe_tile, copy_chunk):
    t = h.shape[0]
    m = _merge(xn, a_out, b_out, w["gates"], w["pa"], w["pb"], tm)
    h1, hn, meta, counts = _route(h, m, w["o"], w["g_ffn"], w["router_hi"], w["router_lo"], w["router_b"], tm)
    pos, tile_expert, n_valid, n_tiles = _dispatch_plan(meta, counts, moe_tile)
    tok = jnp.repeat(jnp.arange(t, dtype=I32), 2)
    flat_pos = pos.reshape(2 * t)
    xs = _row_copy(tok, flat_pos, hn, jnp.zeros((n_tiles * moe_tile, D_MODEL), F32), copy_chunk)
    ys = _experts(tile_expert, n_valid, xs, w["e_gate"], w["e_up"], w["e_down"], moe_tile)
    slot_major = jnp.concatenate([pos[:, 0], pos[:, 1]])
    y2 = _row_copy(slot_major, jnp.arange(2 * t, dtype=I32), ys, jnp.zeros((2 * t, D_MODEL), F32), copy_chunk)
    return _ple(h1, y2.reshape(2, t, D_MODEL), meta, p_l, w["g_ple"], w["ple_gate"], w["ple_proj"], w["g_final"], tm)


def kernel(x_prompt, x_sample, cache_k, cache_v, cache_kidx, page_table, p_prompt, p_sample, g_mix, w_in, gmlp_ln_g, gmlp_ln_b, gmlp_ws, gmlp_b, w_pa, w_pb, w_o, g_ffn, w_router_g, b_router_g, w_router_e, b_router_e, w_e_gate, w_e_up, w_e_down, g_ple, w_ple_gate, w_ple_proj, g_final):
    bsz, seq = x_prompt.shape[:2]
    db, ds = x_sample.shape[:2]
    depth = w_in.shape[0]
    n_phys = cache_k.shape[0]
    n_pages = page_table.shape[1]
    past = n_pages * PAGE_SIZE
    tp, tsm = bsz * seq, db * ds
    assert ds <= 8 and tsm % CHUNK == 0 and CHUNK % ds == 0
    ktop_prompt = min(TOPK_MAX, seq // 4)
    ktop_sample = min(TOPK_MAX, (past + ds) // 4)
    pt_flat = page_table.reshape(-1).astype(I32)

    hp = x_prompt.reshape(tp, D_MODEL)
    hs = x_sample.reshape(tsm, D_MODEL)
    outs = {k: [] for k in ("kp", "vp", "kip", "ks", "vs", "kis", "gvs")}
    yp = ys = None
    for i in range(depth):
        o_g = PROJ_USED
        w = {
            "proj": jnp.pad(w_in[i][:, :o_g], ((0, 0), (0, PROJ_COLS - o_g))).astype(BF16),
            "gmlp": w_in[i][:, o_g:o_g + 2 * GMLP_W].astype(BF16),
            "gates": w_in[i][:, o_g + 2 * GMLP_W:].astype(BF16),
            "pa": w_pa[i].astype(BF16), "pb": w_pb[i].astype(BF16), "o": w_o[i].astype(BF16),
            "g_ffn": g_ffn[i].reshape(1, D_MODEL),
            "e_gate": w_e_gate[i].astype(BF16), "e_up": w_e_up[i].astype(BF16), "e_down": w_e_down[i].astype(BF16),
            "g_ple": g_ple[i].reshape(1, D_MODEL), "ple_gate": w_ple_gate[i].astype(BF16),
            "ple_proj": w_ple_proj[i].astype(BF16), "g_final": g_final.reshape(1, D_MODEL),
        }
        wr = jnp.pad(jnp.concatenate([w_router_e[i], w_router_g[i]], axis=1),
                     ((0, 0), (0, LANES - N_EXPERTS - N_GROUPS)))
        w["router_hi"] = wr.astype(BF16)
        w["router_lo"] = (wr - w["router_hi"].astype(F32)).astype(BF16)
        w["router_b"] = jnp.pad(jnp.concatenate([b_router_e[i], b_router_g[i]]),
                                (0, LANES - N_EXPERTS - N_GROUPS)).reshape(1, LANES)
        gmix = g_mix[i].reshape(1, D_MODEL)
        lng = gmlp_ln_g[i].reshape(1, GMLP_W)
        lnb = gmlp_ln_b[i].reshape(1, GMLP_W)

        xn, q, k, v, qi, kiwi = _proj(hp, gmix, w["proj"], 512)
        ki = kiwi[:, :IDX_DIM]
        b_out = _gmlp(xn, w["gmlp"], lng, lnb, gmlp_ws[i], gmlp_b[i].T, 512, False)[0]
        ki2 = jnp.concatenate([ki, ki], axis=1).astype(BF16).reshape(bsz, seq, LANES)
        kg = k.reshape(bsz, seq, KV_HEADS, HEAD_DIM).transpose(0, 2, 1, 3).astype(BF16)
        vg = v.reshape(bsz, seq, KV_HEADS, HEAD_DIM).transpose(0, 2, 1, 3).astype(BF16)
        a_out = _attn_prompt(qi.reshape(bsz, seq, -1), kiwi[:, IDX_DIM:IDX_DIM + IDX_HEADS].reshape(bsz, seq, -1),
                             ki2, q.reshape(bsz, seq, ATT_W), jnp.concatenate([kg, kg], axis=-1),
                             jnp.concatenate([vg, jnp.zeros_like(vg)], axis=-1), ktop_prompt, 256, 256)
        hp, yp = _tail(hp, xn, a_out.reshape(tp, ATT_W), b_out, p_prompt[i].reshape(tp, PLE_DIM).astype(BF16),
                       w, 256, 256, 256)
        outs["kp"].append(k.reshape(bsz, seq, KV_HEADS, HEAD_DIM))
        outs["vp"].append(v.reshape(bsz, seq, KV_HEADS, HEAD_DIM))
        outs["kip"].append(ki.reshape(bsz, seq, IDX_DIM))

        xn, q, k, v, qi, kiwi = _proj(hs, gmix, w["proj"], 512)
        ki = kiwi[:, :IDX_DIM]
        eye = jnp.eye(CHUNK // ds, dtype=F32)
        ws_s = jnp.einsum("ab,gts->gatbs", eye, gmlp_ws[i][:, :ds, :ds]).reshape(GMLP_GROUPS, CHUNK, CHUNK)
        bias_s = jnp.tile(gmlp_b[i][:, :ds].T, (CHUNK // ds, 1))
        b_out, gvn = _gmlp(xn, w["gmlp"], lng, lnb, ws_s, bias_s, 512, True)

        pad_q = lambda x: jnp.pad(x, ((0, 0), (0, 0), (0, 8 - ds), (0, 0)))
        qi_s = pad_q(qi.reshape(db, ds, IDX_HEADS, IDX_DIM).transpose(0, 2, 1, 3)).reshape(db, IDX_HEADS * 8, IDX_DIM)
        wi_s = pad_q(kiwi[:, IDX_DIM:IDX_DIM + IDX_HEADS].reshape(db, ds, IDX_HEADS, 1).transpose(0, 2, 1, 3))
        pad_new = lambda x: jnp.pad(x.reshape(db, ds, -1), ((0, 0), (0, NEW_PAD - ds), (0, 0))).astype(BF16)
        keys = _sample_index(pt_flat, qi_s, wi_s.reshape(db, IDX_HEADS * 8, 1), pad_new(ki),
                             cache_kidx[:, i].reshape(n_phys, PAGE_SIZE, IDX_DIM), 4, ds)
        bias = _sample_select(keys.reshape(db * 8, past + NEW_PAD), ktop_sample, 256)
        q5 = q.reshape(db, ds, KV_HEADS, Q_PER_KV, 1, HEAD_DIM)
        qbd = (q5 * jnp.eye(KV_HEADS, dtype=BF16)[None, None, :, None, :, None]).reshape(db, ds * N_HEADS, KV_W)
        a_out = _sample_attn(pt_flat, qbd, pad_new(k), pad_new(v), bias.reshape(db, 8, past + NEW_PAD),
                             cache_k[:, i].reshape(n_phys, PAGE_SIZE, KV_W),
                             cache_v[:, i].reshape(n_phys, PAGE_SIZE, KV_W), 2, ds)
        hs, ys = _tail(hs, xn, a_out.reshape(tsm, ATT_W).astype(BF16), b_out,
                       p_sample[i].reshape(tsm, PLE_DIM).astype(BF16), w, 256, 128, 256)
        outs["ks"].append(k.reshape(db, ds, KV_HEADS, HEAD_DIM))
        outs["vs"].append(v.reshape(db, ds, KV_HEADS, HEAD_DIM))
        outs["kis"].append(ki.reshape(db, ds, IDX_DIM))
        outs["gvs"].append(gvn.reshape(db, ds, GMLP_W))

    st = lambda name: jnp.stack(outs[name], axis=1)
    return (yp.reshape(bsz, seq, D_MODEL), ys.reshape(db, ds, D_MODEL), st("kp"), st("vp"), st("kip"),
            st("ks"), st("vs"), st("kis"), st("gvs"))
```

```python
import functools

import jax
import jax.numpy as jnp
from jax import lax
from jax.experimental import pallas as pl
from jax.experimental.pallas import tpu as pltpu

F32 = jnp.float32
BF16 = jnp.bfloat16
I32 = jnp.int32

D_MODEL = 2048
HEAD_DIM = 64
N_HEADS = 16
KV_HEADS = 4
Q_PER_KV = N_HEADS // KV_HEADS
ATT_W = N_HEADS * HEAD_DIM
KV_W = KV_HEADS * HEAD_DIM
ATT_SCALE = HEAD_DIM ** -0.5
IDX_HEADS = 8
IDX_DIM = 64
IDX_SCALE = (IDX_HEADS * IDX_DIM) ** -0.5
TOPK_MAX = 256
CHUNK = 128
GMLP_W = D_MODEL // 2
GMLP_GROUPS = 8
N_GROUPS = 4
EXP_PER_GROUP = 4
N_EXPERTS = N_GROUPS * EXP_PER_GROUP
EXPERT_FF = D_MODEL // 4
PLE_DIM = 256
EPS = 1e-6
PAGE_SIZE = 128

LANES = 128
VMEM_LIMIT_BYTES = 56 * 2 ** 20
MASKED = -1e30
INT_MIN = -2 ** 31
PROJ_COLS = 17 * LANES
PROJ_USED = ATT_W + 2 * KV_W + IDX_HEADS * IDX_DIM + IDX_DIM + IDX_HEADS
NEW_PAD = LANES

_NT = (((1,), (1,)), ((), ()))


def _params(n_axes):
    return pltpu.CompilerParams(dimension_semantics=("arbitrary",) * n_axes,
                                vmem_limit_bytes=VMEM_LIMIT_BYTES)


def _const_spec(shape):
    return pl.BlockSpec(shape, lambda *_: (0,) * len(shape), pipeline_mode=pl.Buffered(1))


def _rmsnorm(x, g):
    return (x * lax.rsqrt(jnp.mean(x * x, axis=-1, keepdims=True) + EPS)) * g


def _gelu(x):
    return x * (0.5 * (1.0 + jnp.tanh(0.7978845608028654 * (x + 0.044715 * (x * x * x)))))


def _sigmoid(x):
    return 1.0 / (1.0 + jnp.exp(-x))


def _dot(a, b):
    return jnp.dot(a, b, preferred_element_type=F32)


def _dot_nt(a, b):
    return lax.dot_general(a, b, _NT, preferred_element_type=F32)


def _proj_kernel(x_ref, g_ref, w_ref, xn_ref, q_ref, k_ref, v_ref, qi_ref, kiwi_ref):
    xnb = _rmsnorm(x_ref[...], g_ref[...]).astype(BF16)
    xn_ref[...] = xnb
    o = 0
    q_ref[...] = (_dot(xnb, w_ref[:, o:o + ATT_W]) * ATT_SCALE).astype(BF16)
    o += ATT_W
    k_ref[...] = _dot(xnb, w_ref[:, o:o + KV_W])
    o += KV_W
    v_ref[...] = _dot(xnb, w_ref[:, o:o + KV_W])
    o += KV_W
    qi_ref[...] = _dot(xnb, w_ref[:, o:o + IDX_HEADS * IDX_DIM]).astype(BF16)
    o += IDX_HEADS * IDX_DIM
    kiwi_ref[...] = _dot(xnb, w_ref[:, o:o + LANES])


def _proj(x, g, w, tm):
    t = x.shape[0]
    row = lambda n: pl.BlockSpec((tm, n), lambda i: (i, 0))
    return pl.pallas_call(
        _proj_kernel,
        grid=(t // tm,),
        in_specs=[row(D_MODEL), _const_spec((1, D_MODEL)), _const_spec((D_MODEL, PROJ_COLS))],
        out_specs=[row(D_MODEL), row(ATT_W), row(KV_W), row(KV_W), row(IDX_HEADS * IDX_DIM), row(LANES)],
        out_shape=[jax.ShapeDtypeStruct((t, D_MODEL), BF16), jax.ShapeDtypeStruct((t, ATT_W), BF16),
                   jax.ShapeDtypeStruct((t, KV_W), F32), jax.ShapeDtypeStruct((t, KV_W), F32),
                   jax.ShapeDtypeStruct((t, IDX_HEADS * IDX_DIM), BF16), jax.ShapeDtypeStruct((t, LANES), F32)],
        compiler_params=_params(1), name="proj",
    )(x, g, w)


def _gmlp_kernel(xn_ref, w_ref, lng_ref, lnb_ref, ws_ref, bias_ref, out_ref, *gvn_ref, n_chunks):
    xnb = xn_ref[...]
    gu = _gelu(_dot(xnb, w_ref[:, :GMLP_W]))
    gv = _gelu(_dot(xnb, w_ref[:, GMLP_W:]))
    xc = gv - jnp.mean(gv, axis=-1, keepdims=True)
    vn = xc * lax.rsqrt(jnp.mean(xc * xc, axis=-1, keepdims=True) + EPS) * lng_ref[...] + lnb_ref[...]
    if gvn_ref:
        gvn_ref[0][...] = vn
    vnb = vn.astype(BF16)
    r = lax.broadcasted_iota(I32, (CHUNK, CHUNK), 0)
    c = lax.broadcasted_iota(I32, (CHUNK, CHUNK), 1)
    gd = GMLP_W // GMLP_GROUPS
    for g in range(GMLP_GROUPS):
        wg = jnp.where(c <= r, ws_ref[g], 0.0).astype(BF16)
        rhs = jnp.concatenate([vnb[ch * CHUNK:(ch + 1) * CHUNK, g * gd:(g + 1) * gd] for ch in range(n_chunks)],
                              axis=1)
        mixed = _dot(wg, rhs)
        b = bias_ref[:, g:g + 1]
        for ch in range(n_chunks):
            rs = slice(ch * CHUNK, (ch + 1) * CHUNK)
            out_ref[rs, g * gd:(g + 1) * gd] = (gu[rs, g * gd:(g + 1) * gd]
                                                * (mixed[:, ch * gd:(ch + 1) * gd] + b)).astype(BF16)


def _gmlp(xn, w, lng, lnb, ws, bias, tm, want_vn):
    t = xn.shape[0]
    row = lambda n: pl.BlockSpec((tm, n), lambda i: (i, 0))
    out_specs = [row(GMLP_W)]
    out_shape = [jax.ShapeDtypeStruct((t, GMLP_W), BF16)]
    if want_vn:
        out_specs.append(row(GMLP_W))
        out_shape.append(jax.ShapeDtypeStruct((t, GMLP_W), F32))
    return pl.pallas_call(
        functools.partial(_gmlp_kernel, n_chunks=tm // CHUNK),
        grid=(t // tm,),
        in_specs=[row(D_MODEL), _const_spec((D_MODEL, 2 * GMLP_W)), _const_spec((1, GMLP_W)),
                  _const_spec((1, GMLP_W)), _const_spec((GMLP_GROUPS, CHUNK, CHUNK)),
                  _const_spec((CHUNK, GMLP_GROUPS))],
        out_specs=out_specs, out_shape=out_shape,
        compiler_params=_params(1), name="gmlp",
    )(xn, w, lng, lnb, ws, bias)


def _ordered_float(u):
    s = u ^ jnp.int32(INT_MIN)
    return lax.bitcast_convert_type(jnp.where(s >= 0, s, s ^ jnp.int32(0x7FFFFFFF)), F32)


def _select_topk(count, bc, ktop, n_keys, ju_ref):
    shape = ju_ref.shape

    def vstep(i, tu):
        cand_u = tu | jnp.left_shift(jnp.int32(1), 31 - i)
        cand = bc(_ordered_float(cand_u))
        cnt = count(lambda blk, idx: blk >= cand)
        return jnp.where(cnt >= ktop, cand_u, tu)

    tu = lax.fori_loop(0, 32, vstep, jnp.zeros(shape, I32))
    unbounded = tu == 0
    thr = jnp.where(unbounded, -jnp.inf, _ordered_float(tu))
    tb = bc(thr)
    cnt_gt = count(lambda blk, idx: blk > tb)
    cnt_ge = count(lambda blk, idx: blk >= tb)
    need = ktop - cnt_gt
    tied = jnp.logical_and(cnt_ge > ktop, jnp.logical_not(unbounded))
    ju_ref[...] = jnp.full(shape, n_keys, I32)

    @pl.when(jnp.max(jnp.where(tied, 1.0, 0.0)) > 0.0)
    def _():
        nbits = (n_keys - 1).bit_length()

        def jstep(i, ju):
            cand = ju | jnp.left_shift(jnp.int32(1), nbits - 1 - i)
            cb_ = bc(cand)
            below = count(lambda blk, idx: jnp.logical_and(blk == tb, idx < cb_))
            return jnp.where(below < need, cand, ju)

        ju = lax.fori_loop(0, nbits, jstep, jnp.zeros(shape, I32))
        ju_ref[...] = jnp.where(tied, ju, n_keys)

    return thr


def _selected(blk, idx, thr, ju):
    return jnp.logical_or(blk > thr, jnp.logical_and(blk == thr, idx <= ju))


def _attn_prompt_kernel(qi_ref, wi_ref, ki_ref, q_ref, kk_ref, vt_ref, o_ref,
                        sc_ref, bias_ref, ju_ref, m_ref, l_ref, acc_ref, *, tq, kb, ktop, seq):
    t0 = pl.program_id(1) * tq
    nkb = (pl.program_id(1) + 1) * (tq // kb)
    query = t0 + lax.broadcasted_iota(I32, (kb, tq), 1)
    key_in_blk = lax.broadcasted_iota(I32, (kb, tq), 0)
    low_half = lax.broadcasted_iota(I32, (tq, LANES), 1) < HEAD_DIM

    wi = wi_ref[...] * IDX_SCALE

    def score_body(cb, _):
        off = pl.multiple_of(cb * kb, kb)
        kblk = ki_ref[pl.ds(off, kb), :]
        acc = jnp.zeros((kb, tq), F32)
        for p in range(IDX_HEADS // 2):
            slab = qi_ref[:, p * LANES:(p + 1) * LANES]
            for half in range(2):
                qh = jnp.where(low_half if half == 0 else jnp.logical_not(low_half), slab, jnp.zeros_like(slab))
                h = 2 * p + half
                acc = acc + wi[h:h + 1, :] * jnp.maximum(_dot_nt(kblk, qh), 0.0)
        sc_ref[pl.ds(off, kb), :] = jnp.where(off + key_in_blk <= query, acc, -jnp.inf)
        return 0

    lax.fori_loop(0, nkb, score_body, 0)

    def count(pred):
        def body(cb, acc):
            off = pl.multiple_of(cb * kb, kb)
            hit = jnp.where(pred(sc_ref[pl.ds(off, kb), :], off + key_in_blk), 1.0, 0.0)
            return acc + jnp.sum(hit.reshape(kb // 8, 8, tq), axis=0)
        return jnp.sum(lax.fori_loop(0, nkb, body, jnp.zeros((8, tq), F32)), axis=0, keepdims=True)

    thr = _select_topk(count, lambda x: jnp.broadcast_to(x, (kb, tq)), ktop, seq, ju_ref)
    ju = ju_ref[...]

    def bias_body(cb, _):
        off = pl.multiple_of(cb * kb, kb)
        key = off + key_in_blk
        sel = jnp.logical_and(_selected(sc_ref[pl.ds(off, kb), :], key, thr, ju), key <= query)
        bias_ref[pl.ds(off, kb), :] = jnp.where(sel, 0.0, MASKED)
        return 0

    lax.fori_loop(0, nkb, bias_body, 0)

    for g in range(KV_HEADS):
        parts = []
        for hh in range(Q_PER_KV):
            hd = Q_PER_KV * g + hh
            slab = q_ref[:, (hd // 2) * LANES:(hd // 2 + 1) * LANES]
            keep = low_half if hd % 2 == 0 else jnp.logical_not(low_half)
            parts.append(jnp.where(keep, slab, jnp.zeros_like(slab)))
        qg = jnp.concatenate(parts, axis=0)
        m_ref[...] = jnp.full(m_ref.shape, MASKED, F32)
        l_ref[...] = jnp.zeros(l_ref.shape, F32)
        acc_ref[...] = jnp.zeros(acc_ref.shape, F32)

        def kv_body(cb, _):
            off = pl.multiple_of(cb * kb, kb)
            b = bias_ref[pl.ds(off, kb), :]
            s = _dot_nt(kk_ref[g, pl.ds(off, kb), :], qg) + jnp.concatenate([b] * Q_PER_KV, axis=1)
            m_old = m_ref[...]
            m_new = jnp.maximum(m_old, jnp.max(s, axis=0, keepdims=True))
            p = jnp.exp(s - m_new)
            alpha = jnp.exp(m_old - m_new)
            l_ref[...] = alpha * l_ref[...] + jnp.sum(p, axis=0, keepdims=True)
            acc_ref[...] = alpha * acc_ref[...] + _dot(vt_ref[g, :, pl.ds(off, kb)], p.astype(BF16))
            m_ref[...] = m_new
            return 0

        lax.fori_loop(0, nkb, kv_body, 0)
        out = acc_ref[...] / l_ref[...]
        for hh in range(Q_PER_KV):
            hd = Q_PER_KV * g + hh
            o_ref[hd * HEAD_DIM:(hd + 1) * HEAD_DIM, :] = out[:, hh * tq:(hh + 1) * tq].astype(BF16)


def _attn_prompt(qi, wi_t, ki2, q, kk, vt, ktop, tq, kb):
    b, s = q.shape[:2]
    blk = lambda n: pl.BlockSpec((None, tq, n), lambda i, j: (i, j, 0))
    return pl.pallas_call(
        functools.partial(_attn_prompt_kernel, tq=tq, kb=kb, ktop=ktop, seq=s),
        grid=(b, s // tq),
        in_specs=[blk(IDX_HEADS * IDX_DIM),
                  pl.BlockSpec((None, IDX_HEADS, tq), lambda i, j: (i, 0, j)),
                  pl.BlockSpec((None, s, LANES), lambda i, j: (i, 0, 0)),
                  blk(ATT_W),
                  pl.BlockSpec((None, KV_HEADS, s, LANES), lambda i, j: (i, 0, 0, 0)),
                  pl.BlockSpec((None, KV_HEADS, HEAD_DIM, s), lambda i, j: (i, 0, 0, 0))],
        out_specs=pl.BlockSpec((None, ATT_W, tq), lambda i, j: (i, 0, j)),
        out_shape=jax.ShapeDtypeStruct((b, ATT_W, s), BF16),
        scratch_shapes=[pltpu.VMEM((s, tq), F32), pltpu.VMEM((s, tq), F32), pltpu.VMEM((1, tq), I32),
                        pltpu.VMEM((1, Q_PER_KV * tq), F32), pltpu.VMEM((1, Q_PER_KV * tq), F32),
                        pltpu.VMEM((HEAD_DIM, Q_PER_KV * tq), F32)],
        compiler_params=_params(2), name="attn_prompt",
    )(qi, wi_t, ki2, q, kk, vt)


def _page_copies(pt_ref, cache, buf, sem, step, slot, sb, n_pages, layer):
    copies = []
    for s in range(sb):
        for p in range(n_pages):
            page = pt_ref[(step * sb + s) * n_pages + p]
            copies.append(pltpu.make_async_copy(cache.at[page, layer],
                                                buf.at[slot, s, pl.ds(p * PAGE_SIZE, PAGE_SIZE)], sem.at[slot]))
    return copies


def _prefetch_pages(pt_ref, caches, bufs, sems, sb, n_pages, layer):
    i = pl.program_id(0)
    n = pl.num_programs(0)
    slot = lax.rem(i, 2)

    @pl.when(i == 0)
    def _():
        for cache, buf, sem in zip(caches, bufs, sems):
            for cp in _page_copies(pt_ref, cache, buf, sem, 0, 0, sb, n_pages, layer):
                cp.start()

    @pl.when(i + 1 < n)
    def _():
        for cache, buf, sem in zip(caches, bufs, sems):
            for cp in _page_copies(pt_ref, cache, buf, sem, i + 1, 1 - slot, sb, n_pages, layer):
                cp.start()

    for cache, buf, sem in zip(caches, bufs, sems):
        for cp in _page_copies(pt_ref, cache, buf, sem, i, slot, sb, n_pages, layer):
            cp.wait()
    return slot


def _sample_index_kernel(pt_ref, qi_ref, wi_ref, kin_ref, cache_ref, sc_ref, buf, sem, *, sb, n_pages, ds, layer):
    slot = _prefetch_pages(pt_ref, (cache_ref,), (buf,), (sem,), sb, n_pages, layer)
    past = n_pages * PAGE_SIZE
    rows = lax.broadcasted_iota(I32, (8, NEW_PAD), 0)
    lane = lax.broadcasted_iota(I32, (8, NEW_PAD), 1)
    real_row = lax.broadcasted_iota(I32, (8, past), 0) < ds
    for s in range(sb):
        qs = qi_ref[s]
        w = wi_ref[s] * IDX_SCALE
        x_past = jnp.maximum(_dot_nt(qs, buf[slot, s].astype(BF16)), 0.0) * w
        x_new = jnp.maximum(_dot_nt(qs, kin_ref[s]), 0.0) * w
        sc_past = x_past[0:8]
        sc_new = x_new[0:8]
        for h in range(1, IDX_HEADS):
            sc_past = sc_past + x_past[8 * h:8 * h + 8]
            sc_new = sc_new + x_new[8 * h:8 * h + 8]
        sc_ref[s, :, :past] = jnp.where(real_row, sc_past, -jnp.inf)
        sc_ref[s, :, past:] = jnp.where(jnp.logical_and(lane <= rows, rows < ds), sc_new, -jnp.inf)


def _sample_index(pt, qi, wi, kin, cache_kidx, sb, ds, layer):
    db = qi.shape[0]
    n_pages = pt.shape[0] // db
    past = n_pages * PAGE_SIZE
    blk = lambda *shape: pl.BlockSpec((sb,) + shape, lambda i, pt_: (i,) + (0,) * len(shape))
    return pl.pallas_call(
        functools.partial(_sample_index_kernel, sb=sb, n_pages=n_pages, ds=ds, layer=layer),
        grid_spec=pltpu.PrefetchScalarGridSpec(
            num_scalar_prefetch=1, grid=(db // sb,),
            in_specs=[blk(IDX_HEADS * 8, IDX_DIM), blk(IDX_HEADS * 8, 1), blk(NEW_PAD, IDX_DIM),
                      pl.BlockSpec(memory_space=pl.ANY)],
            out_specs=blk(8, past + NEW_PAD),
            scratch_shapes=[pltpu.VMEM((2, sb, past, IDX_DIM), F32), pltpu.SemaphoreType.DMA((2,))]),
        out_shape=jax.ShapeDtypeStruct((db, 8, past + NEW_PAD), F32),
        compiler_params=_params(1), name="sample_index",
    )(pt, qi, wi, kin, cache_kidx)


def _sample_select_kernel(sc_ref, bias_ref, ju_ref, *, ktop):
    rows, n_cols = sc_ref.shape
    lane = lax.broadcasted_iota(I32, (rows, LANES), 1)
    blocks = [(sc_ref.at[:, cb * LANES:(cb + 1) * LANES], lane + cb * LANES) for cb in range(n_cols // LANES)]

    def count(pred):
        acc = jnp.zeros((rows, LANES), F32)
        for blk_ref, idx in blocks:
            acc = acc + jnp.where(pred(blk_ref[...], idx), 1.0, 0.0)
        return jnp.sum(acc, axis=1, keepdims=True)

    thr = _select_topk(count, lambda x: jnp.broadcast_to(x, (rows, LANES)), ktop, n_cols, ju_ref)
    ju = ju_ref[...]
    for cb, (blk_ref, idx) in enumerate(blocks):
        blk = blk_ref[...]
        sel = jnp.logical_and(_selected(blk, idx, thr, ju), blk > -jnp.inf)
        bias_ref[:, cb * LANES:(cb + 1) * LANES] = jnp.where(sel, 0.0, MASKED)


def _sample_select(scores, ktop, tr):
    r, w = scores.shape
    return pl.pallas_call(
        functools.partial(_sample_select_kernel, ktop=ktop),
        grid=(r // tr,),
        in_specs=[pl.BlockSpec((tr, w), lambda i: (i, 0))],
        out_specs=pl.BlockSpec((tr, w), lambda i: (i, 0)),
        out_shape=jax.ShapeDtypeStruct((r, w), F32),
        scratch_shapes=[pltpu.VMEM((tr, 1), I32)],
        compiler_params=_params(1), name="sample_select",
    )(scores)


def _sample_attn_kernel(pt_ref, q_ref, kn_ref, vn_ref, bias_ref, ck_ref, cv_ref, o_ref,
                        kbuf, vbuf, ksem, vsem, *, sb, n_pages, ds, layer):
    slot = _prefetch_pages(pt_ref, (ck_ref, cv_ref), (kbuf, vbuf), (ksem, vsem), sb, n_pages, layer)
    past = n_pages * PAGE_SIZE
    n_rows = ds * N_HEADS
    row_g = (lax.broadcasted_iota(I32, (n_rows, HEAD_DIM), 0) // Q_PER_KV) % KV_HEADS
    for s in range(sb):
        q = q_ref[s]
        b = bias_ref[s]
        bias = jnp.concatenate([jnp.broadcast_to(b[t:t + 1], (N_HEADS, past + NEW_PAD)) for t in range(ds)], axis=0)
        s_past = _dot_nt(q, kbuf[slot, s].astype(BF16)) + bias[:, :past]
        s_new = _dot_nt(q, kn_ref[s]) + bias[:, past:]
        m = jnp.maximum(jnp.max(s_past, axis=1, keepdims=True), jnp.max(s_new, axis=1, keepdims=True))
        p_past = jnp.exp(s_past - m)
        p_new = jnp.exp(s_new - m)
        l = jnp.sum(p_past, axis=1, keepdims=True) + jnp.sum(p_new, axis=1, keepdims=True)
        o = (_dot(p_past.astype(BF16), vbuf[slot, s].astype(BF16)) + _dot(p_new.astype(BF16), vn_ref[s])) / l
        out = jnp.zeros((n_rows, HEAD_DIM), F32)
        for g in range(KV_HEADS):
            out = jnp.where(row_g == g, o[:, g * HEAD_DIM:(g + 1) * HEAD_DIM], out)
        o_ref[s] = out


def _sample_attn(pt, qbd, kn, vn, bias, cache_k, cache_v, sb, ds, layer):
    db = qbd.shape[0]
    n_pages = pt.shape[0] // db
    past = n_pages * PAGE_SIZE
    blk = lambda *shape: pl.BlockSpec((sb,) + shape, lambda i, pt_: (i,) + (0,) * len(shape))
    return pl.pallas_call(
        functools.partial(_sample_attn_kernel, sb=sb, n_pages=n_pages, ds=ds, layer=layer),
        grid_spec=pltpu.PrefetchScalarGridSpec(
            num_scalar_prefetch=1, grid=(db // sb,),
            in_specs=[blk(ds * N_HEADS, KV_W), blk(NEW_PAD, KV_W), blk(NEW_PAD, KV_W), blk(8, past + NEW_PAD),
                      pl.BlockSpec(memory_space=pl.ANY), pl.BlockSpec(memory_space=pl.ANY)],
            out_specs=blk(ds * N_HEADS, HEAD_DIM),
            scratch_shapes=[pltpu.VMEM((2, sb, past, KV_W), F32), pltpu.VMEM((2, sb, past, KV_W), F32),
                            pltpu.SemaphoreType.DMA((2,)), pltpu.SemaphoreType.DMA((2,))]),
        out_shape=jax.ShapeDtypeStruct((db, ds * N_HEADS, HEAD_DIM), F32),
        compiler_params=_params(1), name="sample_attn",
    )(pt, qbd, kn, vn, bias, cache_k, cache_v)


def _merge_kernel(xn_ref, a_ref, b_ref, wg_ref, wpa_ref, wpb_ref, m_ref):
    xnb = xn_ref[...]
    ga = _sigmoid(_dot(xnb, wg_ref[:, :D_MODEL]))
    m = ga * _dot(a_ref[...], wpa_ref[...])
    gb = _sigmoid(_dot(xnb, wg_ref[:, D_MODEL:]))
    m_ref[...] = (m + gb * _dot(b_ref[...], wpb_ref[...])).astype(BF16)


def _merge(xn, a, b, wg, wpa, wpb, tm):
    t = xn.shape[0]
    row = lambda n: pl.BlockSpec((tm, n), lambda i: (i, 0))
    return pl.pallas_call(
        _merge_kernel,
        grid=(t // tm,),
        in_specs=[row(D_MODEL), row(ATT_W), row(GMLP_W), _const_spec((D_MODEL, 2 * D_MODEL)),
                  _const_spec((ATT_W, D_MODEL)), _const_spec((GMLP_W, D_MODEL))],
        out_specs=row(D_MODEL),
        out_shape=jax.ShapeDtypeStruct((t, D_MODEL), BF16),
        compiler_params=_params(1), name="merge",
    )(xn, a, b, wg, wpa, wpb)


def _route_kernel(h_ref, m_ref, wo_ref, g_ref, wrh_ref, wrl_ref, br_ref,
                  h1_ref, hn_ref, meta_ref, cnt_ref, run_ref, *, tm):
    @pl.when(pl.program_id(0) == 0)
    def _():
        run_ref[...] = jnp.zeros(run_ref.shape, F32)

    h1 = h_ref[...] + _dot(m_ref[...], wo_ref[...])
    h1_ref[...] = h1
    hn = _rmsnorm(h1, g_ref[...])
    hn_ref[...] = hn
    hn_hi = hn.astype(BF16)
    hn_lo = (hn - hn_hi.astype(F32)).astype(BF16)
    lg = _dot(hn_hi, wrh_ref[...]) + _dot(hn_lo, wrh_ref[...]) + _dot(hn_hi, wrl_ref[...]) + br_ref[...]

    lane = lax.broadcasted_iota(I32, (tm, LANES), 1)
    ninf = jnp.float32(-jnp.inf)
    rmax = lambda x: jnp.max(x, axis=1, keepdims=True)
    rmin = lambda x: jnp.min(x, axis=1, keepdims=True)
    lane_f = lane.astype(F32)
    far = jnp.float32(LANES)
    glog = jnp.where(jnp.logical_and(lane >= N_EXPERTS, lane < N_EXPERTS + N_GROUPS), lg, ninf)
    gmax = rmax(glog)
    grp = rmin(jnp.where(glog == gmax, lane_f, far)) - N_EXPERTS
    pg = 1.0 / jnp.sum(jnp.exp(glog - gmax), axis=1, keepdims=True)
    lane_grp = jnp.right_shift(lane, EXP_PER_GROUP.bit_length() - 1).astype(F32)
    le = jnp.where(jnp.logical_and(lane < N_EXPERTS, lane_grp == grp), lg, ninf)
    v0 = rmax(le)
    i0 = rmin(jnp.where(le == v0, lane_f, far))
    le1 = jnp.where(lane_f == i0, ninf, le)
    v1 = rmax(le1)
    i1 = rmin(jnp.where(le1 == v1, lane_f, far))
    e1 = jnp.exp(v1 - v0)
    w0 = pg / (1.0 + e1)
    w1 = pg * e1 / (1.0 + e1)

    sel0 = lane_f == i0
    sel1 = lane_f == i1
    sel = jnp.where(jnp.logical_or(sel0, sel1), 1.0, 0.0)
    r = lax.broadcasted_iota(I32, (tm, tm), 0)
    c = lax.broadcasted_iota(I32, (tm, tm), 1)
    before = jnp.where(c < r, 1.0, 0.0).astype(BF16)
    rank = _dot(before, sel.astype(BF16)) + run_ref[...]
    rank0 = jnp.sum(jnp.where(sel0, rank, 0.0), axis=1, keepdims=True)
    rank1 = jnp.sum(jnp.where(sel1, rank, 0.0), axis=1, keepdims=True)
    run_ref[...] = run_ref[...] + jnp.sum(sel, axis=0, keepdims=True)
    cnt_ref[...] = run_ref[...]

    meta = jnp.where(lane == 0, i0, 0.0)
    meta = jnp.where(lane == 1, i1, meta)
    meta = jnp.where(lane == 2, rank0, meta)
    meta = jnp.where(lane == 3, rank1, meta)
    meta = jnp.where(lane == 4, w0, meta)
    meta_ref[...] = jnp.where(lane == 5, w1, meta)


def _route(h, m, wo, g, wrh, wrl, br, tm):
    t = h.shape[0]
    row = lambda n: pl.BlockSpec((tm, n), lambda i: (i, 0))
    return pl.pallas_call(
        functools.partial(_route_kernel, tm=tm),
        grid=(t // tm,),
        in_specs=[row(D_MODEL), row(D_MODEL), _const_spec((D_MODEL, D_MODEL)), _const_spec((1, D_MODEL)),
                  _const_spec((D_MODEL, LANES)), _const_spec((D_MODEL, LANES)), _const_spec((1, LANES))],
        out_specs=[row(D_MODEL), row(D_MODEL), row(LANES), pl.BlockSpec((1, LANES), lambda i: (0, 0))],
        out_shape=[jax.ShapeDtypeStruct((t, D_MODEL), F32), jax.ShapeDtypeStruct((t, D_MODEL), F32),
                   jax.ShapeDtypeStruct((t, LANES), F32), jax.ShapeDtypeStruct((1, LANES), F32)],
        scratch_shapes=[pltpu.VMEM((1, LANES), F32)],
        compiler_params=_params(1), name="route",
    )(h, m, wo, g, wrh, wrl, br)


def _dispatch_kernel(pos_ref, x_ref, dst_in_ref, dst_ref, sem, *, tm):
    del dst_in_ref

    def row_copy(j, k):
        return pltpu.make_async_copy(x_ref.at[pl.ds(j, 1)], dst_ref.at[pl.ds(pos_ref[0, 2 * j + k], 1)], sem)

    def start(j, _):
        row_copy(j, 0).start()
        row_copy(j, 1).start()
        return 0

    lax.fori_loop(0, tm, start, 0)

    def wait(j, _):
        row_copy(0, 0).wait()
        row_copy(0, 1).wait()
        return 0

    lax.fori_loop(0, tm, wait, 0)


def _dispatch(pos, x, dst_init, tm):
    t = x.shape[0]
    return pl.pallas_call(
        functools.partial(_dispatch_kernel, tm=tm),
        grid=(t // tm,),
        in_specs=[pl.BlockSpec((None, 1, 2 * tm), lambda i: (i, 0, 0), memory_space=pltpu.SMEM),
                  pl.BlockSpec((tm, D_MODEL), lambda i: (i, 0)), pl.BlockSpec(memory_space=pl.ANY)],
        out_specs=pl.BlockSpec(memory_space=pl.ANY),
        out_shape=jax.ShapeDtypeStruct(dst_init.shape, dst_init.dtype),
        scratch_shapes=[pltpu.SemaphoreType.DMA(())],
        input_output_aliases={2: 0},
        compiler_params=_params(1), name="dispatch",
    )(pos.reshape(t // tm, 1, 2 * tm), x, dst_init)


def _experts_kernel(te_ref, nv_ref, x_ref, wg_ref, wu_ref, wd_ref, y_ref):
    del te_ref

    @pl.when(pl.program_id(0) < nv_ref[0])
    def _():
        x = x_ref[...].astype(BF16)
        hg = _dot(x, wg_ref[...])
        hu = _dot(x, wu_ref[...])
        y_ref[...] = _dot((hg * _sigmoid(hg) * hu).astype(BF16), wd_ref[...])

    @pl.when(pl.program_id(0) >= nv_ref[0])
    def _():
        y_ref[...] = jnp.zeros(y_ref.shape, F32)


def _experts(tile_expert, n_valid, xs, wg, wu, wd, tm):
    p = xs.shape[0]
    wspec = lambda a, b: pl.BlockSpec((None, a, b), lambda i, te, nv: (te[i], 0, 0))
    return pl.pallas_call(
        _experts_kernel,
        grid_spec=pltpu.PrefetchScalarGridSpec(
            num_scalar_prefetch=2, grid=(p // tm,),
            in_specs=[pl.BlockSpec((tm, D_MODEL), lambda i, te, nv: (i, 0)),
                      wspec(D_MODEL, EXPERT_FF), wspec(D_MODEL, EXPERT_FF), wspec(EXPERT_FF, D_MODEL)],
            out_specs=pl.BlockSpec((tm, D_MODEL), lambda i, te, nv: (i, 0))),
        out_shape=jax.ShapeDtypeStruct((p, D_MODEL), F32),
        compiler_params=_params(1), name="experts",
    )(tile_expert, n_valid, xs, wg, wu, wd)


def _ple_kernel(pos_ref, pos_next_ref, h1_ref, ys_ref, meta_ref, p_ref, gp_ref, wg_ref, wp_ref, gf_ref,
                h3_ref, out_ref, ybuf, sem, *, tm):
    i = pl.program_id(0)
    slot = lax.rem(i, 2)

    def row_copy(pref, sl, j, k):
        return pltpu.make_async_copy(ys_ref.at[pl.ds(pref[0, 2 * j + k], 1)], ybuf.at[sl, k, pl.ds(j, 1)],
                                     sem.at[sl])

    def start_rows(pref, sl):
        def body(j, _):
            row_copy(pref, sl, j, 0).start()
            row_copy(pref, sl, j, 1).start()
            return 0
        lax.fori_loop(0, tm, body, 0)

    @pl.when(i == 0)
    def _():
        start_rows(pos_ref, 0)

    @pl.when(i + 1 < pl.num_programs(0))
    def _():
        start_rows(pos_next_ref, 1 - slot)

    def wait(j, _):
        row_copy(pos_ref, slot, 0, 0).wait()
        row_copy(pos_ref, slot, 0, 1).wait()
        return 0

    lax.fori_loop(0, tm, wait, 0)

    meta = meta_ref[...]
    h2 = h1_ref[...] + meta[:, 4:5] * ybuf[slot, 0] + meta[:, 5:6] * ybuf[slot, 1]
    gate = _sigmoid(_dot(_rmsnorm(h2, gp_ref[...]).astype(BF16), wg_ref[...]))
    h3 = h2 + gate * _dot(p_ref[...], wp_ref[...])
    h3_ref[...] = h3
    out_ref[...] = _rmsnorm(h3, gf_ref[...])


def _ple(pos, h1, ys, meta, p, gp, wg, wp, gf, tm):
    t = h1.shape[0]
    nb = t // tm
    row = lambda n: pl.BlockSpec((tm, n), lambda i: (i, 0))
    pos3 = pos.reshape(nb, 1, 2 * tm)
    return pl.pallas_call(
        functools.partial(_ple_kernel, tm=tm),
        grid=(nb,),
        in_specs=[pl.BlockSpec((None, 1, 2 * tm), lambda i: (i, 0, 0), memory_space=pltpu.SMEM),
                  pl.BlockSpec((None, 1, 2 * tm), lambda i: (jnp.minimum(i + 1, nb - 1), 0, 0),
                               memory_space=pltpu.SMEM),
                  row(D_MODEL), pl.BlockSpec(memory_space=pl.ANY), row(LANES), row(PLE_DIM),
                  _const_spec((1, D_MODEL)), _const_spec((D_MODEL, D_MODEL)), _const_spec((PLE_DIM, D_MODEL)),
                  _const_spec((1, D_MODEL))],
        out_specs=[row(D_MODEL), row(D_MODEL)],
        out_shape=[jax.ShapeDtypeStruct((t, D_MODEL), F32), jax.ShapeDtypeStruct((t, D_MODEL), F32)],
        scratch_shapes=[pltpu.VMEM((2, 2, tm, D_MODEL), F32), pltpu.SemaphoreType.DMA((2,))],
        compiler_params=_params(1), name="ple",
    )(pos3, pos3, h1, ys, meta, p, gp, wg, wp, gf)


def _dispatch_plan(meta, counts, tile):
    t = meta.shape[0]
    e = meta[:, 0:2].astype(I32)
    rank = meta[:, 2:4].astype(I32)
    cnt = counts[0, :N_EXPERTS].astype(I32)
    padded = ((cnt + tile - 1) // tile) * tile
    ends = jnp.cumsum(padded)
    starts = ends - padded
    pos = starts[e] + rank
    n_tiles = (2 * t) // tile + N_EXPERTS
    tile_start = jnp.arange(n_tiles, dtype=I32) * tile
    tile_expert = jnp.minimum(jnp.sum((tile_start[:, None] >= ends[None, :]).astype(I32), axis=1), N_EXPERTS - 1)
    n_valid = (ends[-1] // tile).astype(I32).reshape(1)
    last = tile_expert[jnp.maximum(n_valid[0] - 1, 0)]
    tile_expert = jnp.where(jnp.arange(n_tiles) < n_valid[0], tile_expert, last)
    return pos, tile_expert, n_valid, n_tiles


def _tail(h, xn, a_out, b_out, p_l, w, tm, moe_tile):
    m = _merge(xn, a_out, b_out, w["gates"], w["pa"], w["pb"], tm)
    h1, hn, meta, counts = _route(h, m, w["o"], w["g_ffn"], w["router_hi"], w["router_lo"], w["router_b"], tm)
    pos, tile_expert, n_valid, n_tiles = _dispatch_plan(meta, counts, moe_tile)
    xs = _dispatch(pos, hn, jnp.zeros((n_tiles * moe_tile, D_MODEL), F32), tm)
    ys = _experts(tile_expert, n_valid, xs, w["e_gate"], w["e_up"], w["e_down"], moe_tile)
    return _ple(pos, h1, ys, meta, p_l, w["g_ple"], w["ple_gate"], w["ple_proj"], w["g_final"], tm)


def kernel(x_prompt, x_sample, cache_k, cache_v, cache_kidx, page_table, p_prompt, p_sample, g_mix, w_in, gmlp_ln_g, gmlp_ln_b, gmlp_ws, gmlp_b, w_pa, w_pb, w_o, g_ffn, w_router_g, b_router_g, w_router_e, b_router_e, w_e_gate, w_e_up, w_e_down, g_ple, w_ple_gate, w_ple_proj, g_final):
    bsz, seq = x_prompt.shape[:2]
    db, ds = x_sample.shape[:2]
    depth = w_in.shape[0]
    n_phys = cache_k.shape[0]
    n_pages = page_table.shape[1]
    past = n_pages * PAGE_SIZE
    tp, tsm = bsz * seq, db * ds
    assert ds <= 8 and tsm % CHUNK == 0 and CHUNK % ds == 0
    ktop_prompt = min(TOPK_MAX, seq // 4)
    ktop_sample = min(TOPK_MAX, (past + ds) // 4)
    pt_flat = page_table.reshape(-1).astype(I32)

    hp = x_prompt.reshape(tp, D_MODEL)
    hs = x_sample.reshape(tsm, D_MODEL)
    outs = {k: [] for k in ("kp", "vp", "kip", "ks", "vs", "kis", "gvs")}
    yp = ys = None
    for i in range(depth):
        o_g = PROJ_USED
        w = {
            "proj": jnp.pad(w_in[i][:, :o_g], ((0, 0), (0, PROJ_COLS - o_g))).astype(BF16),
            "gmlp": w_in[i][:, o_g:o_g + 2 * GMLP_W].astype(BF16),
            "gates": w_in[i][:, o_g + 2 * GMLP_W:].astype(BF16),
            "pa": w_pa[i].astype(BF16), "pb": w_pb[i].astype(BF16), "o": w_o[i].astype(BF16),
            "g_ffn": g_ffn[i].reshape(1, D_MODEL),
            "e_gate": w_e_gate[i].astype(BF16), "e_up": w_e_up[i].astype(BF16), "e_down": w_e_down[i].astype(BF16),
            "g_ple": g_ple[i].reshape(1, D_MODEL), "ple_gate": w_ple_gate[i].astype(BF16),
            "ple_proj": w_ple_proj[i].astype(BF16), "g_final": g_final.reshape(1, D_MODEL),
        }
        wr = jnp.pad(jnp.concatenate([w_router_e[i], w_router_g[i]], axis=1),
                     ((0, 0), (0, LANES - N_EXPERTS - N_GROUPS)))
        w["router_hi"] = wr.astype(BF16)
        w["router_lo"] = (wr - w["router_hi"].astype(F32)).astype(BF16)
        w["router_b"] = jnp.pad(jnp.concatenate([b_router_e[i], b_router_g[i]]),
                                (0, LANES - N_EXPERTS - N_GROUPS)).reshape(1, LANES)
        gmix = g_mix[i].reshape(1, D_MODEL)
        lng = gmlp_ln_g[i].reshape(1, GMLP_W)
        lnb = gmlp_ln_b[i].reshape(1, GMLP_W)

        xn, q, k, v, qi, kiwi = _proj(hp, gmix, w["proj"], 512)
        ki = kiwi[:, :IDX_DIM]
        b_out = _gmlp(xn, w["gmlp"], lng, lnb, gmlp_ws[i], gmlp_b[i].T, 512, False)[0]
        ki2 = jnp.concatenate([ki, ki], axis=1).astype(BF16).reshape(bsz, seq, LANES)
        kg = k.reshape(bsz, seq, KV_HEADS, HEAD_DIM).transpose(0, 2, 1, 3).astype(BF16)
        vt = v.reshape(bsz, seq, KV_HEADS, HEAD_DIM).transpose(0, 2, 3, 1).astype(BF16)
        wi_t = kiwi[:, IDX_DIM:IDX_DIM + IDX_HEADS].reshape(bsz, seq, IDX_HEADS).transpose(0, 2, 1)
        a_t = _attn_prompt(qi.reshape(bsz, seq, -1), wi_t, ki2, q.reshape(bsz, seq, ATT_W),
                           jnp.concatenate([kg, kg], axis=-1), vt, ktop_prompt, 256, 256)
        a_out = a_t.transpose(0, 2, 1).reshape(tp, ATT_W)
        hp, yp = _tail(hp, xn, a_out, b_out, p_prompt[i].reshape(tp, PLE_DIM).astype(BF16), w, 256, 256)
        outs["kp"].append(k.reshape(bsz, seq, KV_HEADS, HEAD_DIM))
        outs["vp"].append(v.reshape(bsz, seq, KV_HEADS, HEAD_DIM))
        outs["kip"].append(ki.reshape(bsz, seq, IDX_DIM))

        xn, q, k, v, qi, kiwi = _proj(hs, gmix, w["proj"], 512)
        ki = kiwi[:, :IDX_DIM]
        eye = jnp.eye(CHUNK // ds, dtype=F32)
        ws_s = jnp.einsum("ab,gts->gatbs", eye, gmlp_ws[i][:, :ds, :ds]).reshape(GMLP_GROUPS, CHUNK, CHUNK)
        bias_s = jnp.tile(gmlp_b[i][:, :ds].T, (CHUNK // ds, 1))
        b_out, gvn = _gmlp(xn, w["gmlp"], lng, lnb, ws_s, bias_s, 512, True)

        pad_q = lambda x: jnp.pad(x, ((0, 0), (0, 0), (0, 8 - ds), (0, 0)))
        qi_s = pad_q(qi.reshape(db, ds, IDX_HEADS, IDX_DIM).transpose(0, 2, 1, 3)).reshape(db, IDX_HEADS * 8, IDX_DIM)
        wi_s = pad_q(kiwi[:, IDX_DIM:IDX_DIM + IDX_HEADS].reshape(db, ds, IDX_HEADS, 1).transpose(0, 2, 1, 3))
        pad_new = lambda x: jnp.pad(x.reshape(db, ds, -1), ((0, 0), (0, NEW_PAD - ds), (0, 0))).astype(BF16)
        scores = _sample_index(pt_flat, qi_s, wi_s.reshape(db, IDX_HEADS * 8, 1), pad_new(ki), cache_kidx, 4, ds, i)
        bias = _sample_select(scores.reshape(db * 8, past + NEW_PAD), ktop_sample, 256)
        q5 = q.reshape(db, ds, KV_HEADS, Q_PER_KV, 1, HEAD_DIM)
        qbd = (q5 * jnp.eye(KV_HEADS, dtype=BF16)[None, None, :, None, :, None]).reshape(db, ds * N_HEADS, KV_W)
        a_out = _sample_attn(pt_flat, qbd, pad_new(k), pad_new(v), bias.reshape(db, 8, past + NEW_PAD),
                             cache_k.reshape(n_phys, depth, PAGE_SIZE, KV_W),
                             cache_v.reshape(n_phys, depth, PAGE_SIZE, KV_W), 2, ds, i)
        hs, ys = _tail(hs, xn, a_out.reshape(tsm, ATT_W).astype(BF16), b_out,
                       p_sample[i].reshape(tsm, PLE_DIM).astype(BF16), w, 256, 128)
        outs["ks"].append(k.reshape(db, ds, KV_HEADS, HEAD_DIM))
        outs["vs"].append(v.reshape(db, ds, KV_HEADS, HEAD_DIM))
        outs["kis"].append(ki.reshape(db, ds, IDX_DIM))
        outs["gvs"].append(gvn.reshape(db, ds, GMLP_W))

    st = lambda name: jnp.stack(outs[name], axis=1)
    return (yp.reshape(bsz, seq, D_MODEL), ys.reshape(db, ds, D_MODEL), st("kp"), st("vp"), st("kip"),
            st("ks"), st("vs"), st("kis"), st("gvs"))
```
